```python
import jax, jax.numpy as jnp
from jax import lax
import numpy as np

D_MODEL = 1024
BATCH = 2
SEQ = 16384
DEPTH = 1
DEC_BATCH = 8
DEC_SEQ = 4096
PAST_LEN = 128

PLE_DIM = 256
BLOCK = 128
WINDOW = 128
SWA_HEADS = 8
SWA_KV_HEADS = 2
SWA_GROUP = SWA_HEADS // SWA_KV_HEADS
SWA_HEAD_DIM = 64
MLA_HEADS = 8
MLA_NOPE_DIM = 64
MLA_ROPE_DIM = 32
MLA_V_DIM = 64
MLA_Q_RANK = 256
MLA_KV_RANK = 128
ROPE_THETA = 10000.0
D_FF = 2816
EPS = 1e-6
NEG_INF = -1e30

SWA_Q_W = SWA_HEADS * SWA_HEAD_DIM
SWA_KV_W = SWA_KV_HEADS * SWA_HEAD_DIM
SWA_OUT_W = SWA_HEADS * SWA_HEAD_DIM
MLA_OUT_W = MLA_HEADS * MLA_V_DIM
MIX_W = SWA_OUT_W + MLA_OUT_W
OFF_Q = 0
OFF_K = OFF_Q + SWA_Q_W
OFF_V = OFF_K + SWA_KV_W
OFF_CQ = OFF_V + SWA_KV_W
OFF_CKV = OFF_CQ + MLA_Q_RANK
OFF_KR = OFF_CKV + MLA_KV_RANK
IN_W = OFF_KR + MLA_ROPE_DIM

kernel_name = "hymba_swa_mla_encoder"


def rmsnorm(x, g):
    xf = x.astype(jnp.float32)
    r = lax.rsqrt(jnp.mean(xf * xf, axis=-1, keepdims=True) + EPS)
    return (xf * r).astype(x.dtype) * g


def alibi_slopes():
    s = 2.0 ** (-8.0 * np.arange(1, SWA_HEADS + 1, dtype=np.float32) / SWA_HEADS)
    return jnp.asarray(s, dtype=jnp.float32).reshape(SWA_KV_HEADS, SWA_GROUP)


def rope(x):
    S, D = x.shape[1], x.shape[-1]
    half = D // 2
    inv = ROPE_THETA ** (-jnp.arange(half, dtype=jnp.float32) / half)
    ang = jnp.arange(S, dtype=jnp.float32)[:, None] * inv[None, :]
    shape = (S,) + (1,) * (x.ndim - 3) + (half,)
    cos = jnp.cos(ang).reshape(shape).astype(x.dtype)
    sin = jnp.sin(ang).reshape(shape).astype(x.dtype)
    x1, x2 = x[..., :half], x[..., half:]
    return jnp.concatenate([x1 * cos - x2 * sin, x2 * cos + x1 * sin], axis=-1)


def swa_attention(q, k, v, sink):
    B, S = q.shape[0], q.shape[1]
    nb = S // BLOCK
    qb = q.reshape(B, nb, BLOCK, SWA_KV_HEADS, SWA_GROUP, SWA_HEAD_DIM)
    pad = ((0, 0), (BLOCK, BLOCK), (0, 0), (0, 0))
    kp = jnp.pad(k, pad).reshape(B, nb + 2, BLOCK, SWA_KV_HEADS, SWA_HEAD_DIM)
    vp = jnp.pad(v, pad).reshape(B, nb + 2, BLOCK, SWA_KV_HEADS, SWA_HEAD_DIM)
    kb = jnp.concatenate([kp[:, :-2], kp[:, 1:-1], kp[:, 2:]], axis=2)
    vb = jnp.concatenate([vp[:, :-2], vp[:, 1:-1], vp[:, 2:]], axis=2)
    s = jnp.einsum('bnqkgd,bnckd->bnkgqc', qb, kb).astype(jnp.float32) * (SWA_HEAD_DIM ** -0.5)
    a = jnp.arange(BLOCK)[:, None]
    c = jnp.arange(3 * BLOCK)[None, :]
    rel = c - BLOCK - a
    key_pos = jnp.arange(nb)[:, None] * BLOCK + jnp.arange(3 * BLOCK)[None, :] - BLOCK
    valid = (jnp.abs(rel) <= WINDOW)[None] & ((key_pos >= 0) & (key_pos < S))[:, None, :]
    dist = jnp.abs(rel).astype(jnp.float32)
    s = s - alibi_slopes()[:, :, None, None] * dist
    s = jnp.where(valid[None, :, None, None], s, NEG_INF)
    sink_b = sink.astype(jnp.float32).reshape(SWA_KV_HEADS, SWA_GROUP)[:, :, None, None]
    m = jnp.maximum(jnp.max(s, axis=-1, keepdims=True), sink_b)
    e = jnp.exp(s - m)
    den = jnp.sum(e, axis=-1, keepdims=True) + jnp.exp(sink_b - m)
    o = jnp.einsum('bnkgqc,bnckd->bnqkgd', (e / den).astype(v.dtype), vb)
    return o.reshape(B, S, SWA_OUT_W)


def mla_attention(q_nope, q_rope, k_nope, k_rope, v):
    B, S = q_nope.shape[0], q_nope.shape[1]
    nb = S // BLOCK
    scale = (MLA_NOPE_DIM + MLA_ROPE_DIM) ** -0.5

    def to_blocks(t):
        return jnp.moveaxis(t.reshape((B, nb, BLOCK) + t.shape[2:]), 1, 0)

    def one_block(qs):
        qn, qr = qs
        s = jnp.einsum('bqhd,bshd->bhqs', qn, k_nope) + jnp.einsum('bqhd,bsd->bhqs', qr, k_rope)
        p = jax.nn.softmax(s.astype(jnp.float32) * scale, axis=-1)
        return jnp.einsum('bhqs,bshd->bqhd', p.astype(v.dtype), v)

    o = lax.map(one_block, (to_blocks(q_nope), to_blocks(q_rope)))
    return jnp.moveaxis(o, 0, 1).reshape(B, S, MLA_OUT_W)


def encoder_layer(h, p_i, g_mix, w_in, g_cq, w_uq, g_ckv, w_ukv, sink, g_swa_out, g_mla_out, w_o,
                  g_ffn, w_ffn_gate, w_ffn_up, w_ffn_down, g_ple, w_ple_gate, w_ple_proj):
    B, S = h.shape[0], h.shape[1]
    n = rmsnorm(h, g_mix)
    proj = n @ w_in
    q = proj[..., OFF_Q:OFF_K].reshape(B, S, SWA_HEADS, SWA_HEAD_DIM)
    k = proj[..., OFF_K:OFF_V].reshape(B, S, SWA_KV_HEADS, SWA_HEAD_DIM)
    v = proj[..., OFF_V:OFF_CQ].reshape(B, S, SWA_KV_HEADS, SWA_HEAD_DIM)
    o_swa = swa_attention(q, k, v, sink)
    c_q = rmsnorm(proj[..., OFF_CQ:OFF_CKV], g_cq)
    qm = (c_q @ w_uq).reshape(B, S, MLA_HEADS, MLA_NOPE_DIM + MLA_ROPE_DIM)
    q_nope, q_rope = qm[..., :MLA_NOPE_DIM], rope(qm[..., MLA_NOPE_DIM:])
    c_kv = rmsnorm(proj[..., OFF_CKV:OFF_KR], g_ckv)
    kv = (c_kv @ w_ukv).reshape(B, S, MLA_HEADS, MLA_NOPE_DIM + MLA_V_DIM)
    k_nope, v_m = kv[..., :MLA_NOPE_DIM], kv[..., MLA_NOPE_DIM:]
    k_rope = rope(proj[..., OFF_KR:IN_W])
    o_mla = mla_attention(q_nope, q_rope, k_nope, k_rope, v_m)
    mix = jnp.concatenate([rmsnorm(o_swa, g_swa_out), rmsnorm(o_mla, g_mla_out)], axis=-1) @ w_o
    h = h + mix
    n2 = rmsnorm(h, g_ffn)
    h = h + (jax.nn.silu(n2 @ w_ffn_gate) * (n2 @ w_ffn_up)) @ w_ffn_down
    gate = jax.nn.sigmoid(rmsnorm(h, g_ple) @ w_ple_gate)
    h = h + gate * (p_i @ w_ple_proj)
    return h


def encoder(x, p, g_mix, w_in, g_cq, w_uq, g_ckv, w_ukv, sink, g_swa_out, g_mla_out, w_o,
            g_ffn, w_ffn_gate, w_ffn_up, w_ffn_down, g_ple, w_ple_gate, w_ple_proj, g_final):
    h = x
    for i in range(DEPTH):
        h = encoder_layer(h, p[i], g_mix[i], w_in[i], g_cq[i], w_uq[i], g_ckv[i], w_ukv[i], sink[i],
                          g_swa_out[i], g_mla_out[i], w_o[i], g_ffn[i], w_ffn_gate[i], w_ffn_up[i],
                          w_ffn_down[i], g_ple[i], w_ple_gate[i], w_ple_proj[i])
    return rmsnorm(h, g_final)


def setup_inputs(seed: int = 0) -> dict:
    key = jax.random.key(seed)
    ks = jax.random.split(key, 24)
    f32 = jnp.float32

    def w(k, shape, fan_in):
        return jax.random.normal(k, shape, f32) * (fan_in ** -0.5)

    def g(k, shape):
        return 1.0 + 0.01 * jax.random.normal(k, shape, f32)

    L = DEPTH
    return {
        "x_prompt": jax.random.normal(ks[0], (BATCH, SEQ, D_MODEL), f32),
        "x_sample": jax.random.normal(ks[1], (DEC_BATCH, DEC_SEQ, D_MODEL), f32),
        "p_prompt": jax.random.normal(ks[2], (DEPTH, BATCH, SEQ, PLE_DIM), f32),
        "p_sample": jax.random.normal(ks[3], (DEPTH, DEC_BATCH, DEC_SEQ, PLE_DIM), f32),
        "g_mix": g(ks[4], (L, D_MODEL)),
        "w_in": w(ks[5], (L, D_MODEL, IN_W), D_MODEL),
        "g_cq": g(ks[6], (L, MLA_Q_RANK)),
        "w_uq": w(ks[7], (L, MLA_Q_RANK, MLA_HEADS * (MLA_NOPE_DIM + MLA_ROPE_DIM)), MLA_Q_RANK),
        "g_ckv": g(ks[8], (L, MLA_KV_RANK)),
        "w_ukv": w(ks[9], (L, MLA_KV_RANK, MLA_HEADS * (MLA_NOPE_DIM + MLA_V_DIM)), MLA_KV_RANK),
        "sink": 0.5 * jax.random.normal(ks[10], (L, SWA_HEADS), f32),
        "g_swa_out": g(ks[11], (L, SWA_OUT_W)),
        "g_mla_out": g(ks[12], (L, MLA_OUT_W)),
        "w_o": w(ks[13], (L, MIX_W, D_MODEL), MIX_W),
        "g_ffn": g(ks[14], (L, D_MODEL)),
        "w_ffn_gate": w(ks[15], (L, D_MODEL, D_FF), D_MODEL),
        "w_ffn_up": w(ks[16], (L, D_MODEL, D_FF), D_MODEL),
        "w_ffn_down": w(ks[17], (L, D_FF, D_MODEL), D_FF),
        "g_ple": g(ks[18], (L, D_MODEL)),
        "w_ple_gate": w(ks[19], (L, D_MODEL, D_MODEL), D_MODEL),
        "w_ple_proj": w(ks[20], (L, PLE_DIM, D_MODEL), PLE_DIM),
        "g_final": g(ks[21], (D_MODEL,)),
    }


def reference(x_prompt, x_sample, p_prompt, p_sample, g_mix, w_in, g_cq, w_uq, g_ckv, w_ukv, sink,
              g_swa_out, g_mla_out, w_o, g_ffn, w_ffn_gate, w_ffn_up, w_ffn_down, g_ple, w_ple_gate,
              w_ple_proj, g_final):
    y_prompt = encoder(x_prompt, p_prompt, g_mix, w_in, g_cq, w_uq, g_ckv, w_ukv, sink, g_swa_out,
                       g_mla_out, w_o, g_ffn, w_ffn_gate, w_ffn_up, w_ffn_down, g_ple, w_ple_gate,
                       w_ple_proj, g_final)
    y_sample = encoder(x_sample, p_sample, g_mix, w_in, g_cq, w_uq, g_ckv, w_ukv, sink, g_swa_out,
                       g_mla_out, w_o, g_ffn, w_ffn_gate, w_ffn_up, w_ffn_down, g_ple, w_ple_gate,
                       w_ple_proj, g_final)
    return (y_prompt, y_sample)
```

```python
import functools
import math

import jax
import jax.numpy as jnp
import numpy as np
from jax import lax
from jax.experimental import pallas as pl
from jax.experimental.pallas import tpu as pltpu

F32 = jnp.float32
BF16 = jnp.bfloat16

D_MODEL = 1024
PLE_DIM = 256
WINDOW = 128
SWA_HEADS = 8
SWA_KV_HEADS = 2
SWA_GROUP = SWA_HEADS // SWA_KV_HEADS
SWA_HEAD_DIM = 64
MLA_HEADS = 8
MLA_NOPE_DIM = 64
MLA_ROPE_DIM = 32
MLA_V_DIM = 64
MLA_Q_RANK = 256
MLA_KV_RANK = 128
ROPE_THETA = 10000.0
D_FF = 2816
EPS = 1e-6
NEG_INF = -1e30

SWA_Q_W = SWA_HEADS * SWA_HEAD_DIM
SWA_KV_W = SWA_KV_HEADS * SWA_HEAD_DIM
MLA_OUT_W = MLA_HEADS * MLA_V_DIM
IN_W = SWA_Q_W + 2 * SWA_KV_W + MLA_Q_RANK + MLA_KV_RANK + MLA_ROPE_DIM

LANES = 128
MLA_HEAD_PAD = LANES
MLA_PAD_W = MLA_HEADS * MLA_HEAD_PAD
IN_W_PAD = 1280
OFF_CQ = SWA_Q_W + 2 * SWA_KV_W
OFF_CKV = OFF_CQ + MLA_Q_RANK
OFF_KR_BLK = OFF_CKV + MLA_KV_RANK

TOK_TILE = 512
MLA_TQ = 512
SWA_TQ = 256
POST_TILE = 256
VMEM_LIMIT = 56 * 1024 * 1024

LOG2E = math.log2(math.e)
MLA_QSCALE = (MLA_NOPE_DIM + MLA_ROPE_DIM) ** -0.5 * LOG2E
SWA_QSCALE = SWA_HEAD_DIM ** -0.5


def _rms(x, g):
    r = lax.rsqrt(jnp.mean(x * x, axis=-1, keepdims=True) + EPS)
    return (x * r) * g


def _proj_kernel(x_ref, gmix_ref, win_ref, gcq_ref, wuq_ref, gckv_ref, wuk_ref, wuvt_ref,
                 cos_ref, sina_ref, sinb_ref,
                 qs_ref, ks_ref, vs_ref, qm_ref, km_ref, vt_ref):
    n = _rms(x_ref[...], gmix_ref[...]).astype(BF16)
    proj = jnp.dot(n, win_ref[...], preferred_element_type=F32)
    qs_ref[...] = (proj[:, :SWA_Q_W] * SWA_QSCALE).astype(BF16)
    ks_ref[...] = proj[:, SWA_Q_W:SWA_Q_W + SWA_KV_W].astype(BF16)
    vs_ref[...] = proj[:, SWA_Q_W + SWA_KV_W:OFF_CQ].astype(BF16)

    cq = _rms(proj[:, OFF_CQ:OFF_CKV], gcq_ref[...]).astype(BF16)
    qm = jnp.dot(cq, wuq_ref[...], preferred_element_type=F32)
    ckv = _rms(proj[:, OFF_CKV:OFF_KR_BLK], gckv_ref[...]).astype(BF16)
    kn = jnp.dot(ckv, wuk_ref[...], preferred_element_type=F32)
    vt = lax.dot_general(wuvt_ref[...], ckv, (((1,), (1,)), ((), ())),
                         preferred_element_type=F32)
    vt_ref[...] = vt.astype(BF16)

    cos, sina, sinb = cos_ref[...], sina_ref[...], sinb_ref[...]

    def rope(t):
        return (t * cos + pltpu.roll(t, LANES - MLA_ROPE_DIM // 2, 1) * sina
                + pltpu.roll(t, MLA_ROPE_DIM // 2, 1) * sinb)

    kr = rope(proj[:, OFF_KR_BLK:IN_W_PAD])
    for h in range(MLA_HEADS):
        sl = slice(h * MLA_HEAD_PAD, (h + 1) * MLA_HEAD_PAD)
        qm_ref[:, sl] = (rope(qm[:, sl]) * MLA_QSCALE).astype(BF16)
        km_ref[:, sl] = (kn[:, sl] + kr).astype(BF16)


def _proj_call(x, cos, sina, sinb, w):
    B, S, _ = x.shape
    tm = TOK_TILE
    nt = S // tm
    const = lambda shape: pl.BlockSpec(shape, lambda b, i: (0,) * len(shape))
    tok = lambda width: pl.BlockSpec((None, tm, width), lambda b, i: (b, i, 0))
    tab = pl.BlockSpec((tm, LANES), lambda b, i: (i, 0))
    out_shape = (
        jax.ShapeDtypeStruct((B, S, SWA_Q_W), BF16),
        jax.ShapeDtypeStruct((B, S, SWA_KV_W), BF16),
        jax.ShapeDtypeStruct((B, S, SWA_KV_W), BF16),
        jax.ShapeDtypeStruct((B, S, MLA_PAD_W), BF16),
        jax.ShapeDtypeStruct((B, S, MLA_PAD_W), BF16),
        jax.ShapeDtypeStruct((B, nt, MLA_OUT_W, tm), BF16),
    )
    return pl.pallas_call(
        _proj_kernel,
        grid=(B, nt),
        in_specs=[
            tok(D_MODEL), const((1, D_MODEL)), const((D_MODEL, IN_W_PAD)),
            const((1, MLA_Q_RANK)), const((MLA_Q_RANK, MLA_PAD_W)),
            const((1, MLA_KV_RANK)), const((MLA_KV_RANK, MLA_PAD_W)), const((MLA_OUT_W, MLA_KV_RANK)),
            tab, tab, tab,
        ],
        out_specs=(
            tok(SWA_Q_W), tok(SWA_KV_W), tok(SWA_KV_W), tok(MLA_PAD_W), tok(MLA_PAD_W),
            pl.BlockSpec((None, None, MLA_OUT_W, tm), lambda b, i: (b, i, 0, 0)),
        ),
        out_shape=out_shape,
        compiler_params=pltpu.CompilerParams(
            dimension_semantics=("parallel", "parallel"), vmem_limit_bytes=VMEM_LIMIT),
        name="proj",
    )(x, w["g_mix"], w["w_in"], w["g_cq"], w["w_uq"], w["g_ckv"], w["w_uk"], w["w_uvt"],
      cos, sina, sinb)


def _swa_kernel(sink_ref, q_ref, kp_ref, kc_ref, kn_ref, vp_ref, vc_ref, vn_ref, bias_ref, o_ref):
    i = pl.program_id(1)
    last = pl.num_programs(1) - 1
    kw = jnp.concatenate([kp_ref[...], kc_ref[...], kn_ref[...]], axis=0)
    vw = jnp.concatenate([vp_ref[...], vc_ref[...], vn_ref[...]], axis=0)
    nkeys = kw.shape[0]
    col = lax.broadcasted_iota(jnp.int32, (1, nkeys), 1)
    col_ok = ((col >= WINDOW) | (i > 0)) & ((col < nkeys - WINDOW) | (i < last))
    outs = []
    for h in range(SWA_HEADS):
        g = h // SWA_GROUP
        qh = q_ref[:, h * SWA_HEAD_DIM:(h + 1) * SWA_HEAD_DIM]
        kh = kw[:, g * SWA_HEAD_DIM:(g + 1) * SWA_HEAD_DIM]
        vh = vw[:, g * SWA_HEAD_DIM:(g + 1) * SWA_HEAD_DIM]
        s = lax.dot_general(qh, kh, (((1,), (1,)), ((), ())), preferred_element_type=F32)
        s = jnp.where(col_ok, s + bias_ref[h], NEG_INF)
        sink = sink_ref[h]
        m = jnp.maximum(jnp.max(s, axis=-1, keepdims=True), sink)
        e = jnp.exp(s - m)
        den = jnp.sum(e, axis=-1, keepdims=True) + jnp.exp(sink - m)
        o = jnp.dot(e.astype(BF16), vh, preferred_element_type=F32)
        outs.append(o / den)
    o_ref[...] = jnp.concatenate(outs, axis=-1)


def _swa_bias():
    slopes = 2.0 ** (-8.0 * np.arange(1, SWA_HEADS + 1, dtype=np.float32) / SWA_HEADS)
    a = np.arange(SWA_TQ)[:, None]
    c = np.arange(SWA_TQ + 2 * WINDOW)[None, :]
    dist = np.abs(c - WINDOW - a).astype(np.float32)
    bias = np.where(dist[None] <= WINDOW, -slopes[:, None, None] * dist[None], np.float32(NEG_INF))
    return jnp.asarray(bias, dtype=F32)


def _swa_call(qs, ks, vs, sink):
    B, S, _ = qs.shape
    tq = SWA_TQ
    nq = S // tq
    r = tq // WINDOW
    nhalo = S // WINDOW
    cur = lambda width: pl.BlockSpec((None, tq, width), lambda b, i: (b, i, 0))
    prev = pl.BlockSpec((None, WINDOW, SWA_KV_W), lambda b, i: (b, jnp.maximum(i * r - 1, 0), 0))
    nxt = pl.BlockSpec((None, WINDOW, SWA_KV_W),
                       lambda b, i: (b, jnp.minimum((i + 1) * r, nhalo - 1), 0))
    bias = _swa_bias()
    return pl.pallas_call(
        _swa_kernel,
        grid=(B, nq),
        in_specs=[
            pl.BlockSpec(memory_space=pltpu.SMEM),
            cur(SWA_Q_W), prev, cur(SWA_KV_W), nxt, prev, cur(SWA_KV_W), nxt,
            pl.BlockSpec(bias.shape, lambda b, i: (0, 0, 0)),
        ],
        out_specs=cur(SWA_Q_W),
        out_shape=jax.ShapeDtypeStruct((B, S, SWA_Q_W), F32),
        compiler_params=pltpu.CompilerParams(
            dimension_semantics=("parallel", "parallel"), vmem_limit_bytes=VMEM_LIMIT),
        name="swa",
    )(sink, qs, ks, ks, ks, vs, vs, vs, bias)


def _mla_kernel(q_ref, k_ref, vt_ref, o_ref):
    q = q_ref[...]
    nk, _, tk = vt_ref.shape
    tq = q.shape[0]

    def body(j, carry):
        m, l, acc = carry
        k = k_ref[pl.ds(pl.multiple_of(j * tk, tk), tk), :]
        s = lax.dot_general(k, q, (((1,), (1,)), ((), ())), preferred_element_type=F32)
        m_new = jnp.maximum(m, jnp.max(s, axis=0, keepdims=True))
        p = jnp.exp2(s - m_new)
        alpha = jnp.exp2(m - m_new)
        l = alpha * l + jnp.sum(p, axis=0, keepdims=True)
        acc = alpha * acc + jnp.dot(vt_ref[j], p.astype(BF16), preferred_element_type=F32)
        return m_new, l, acc

    init = (jnp.full((1, tq), NEG_INF, F32), jnp.zeros((1, tq), F32),
            jnp.zeros((MLA_V_DIM, tq), F32))
    _, l, acc = lax.fori_loop(0, nk, body, init)
    o_ref[...] = acc / l


def _mla_call(qm, km, vt):
    B, S, _ = qm.shape
    tq = MLA_TQ
    nk, tk = vt.shape[1], vt.shape[3]
    return pl.pallas_call(
        _mla_kernel,
        grid=(B, MLA_HEADS, S // tq),
        in_specs=[
            pl.BlockSpec((None, tq, MLA_HEAD_PAD), lambda b, h, i: (b, i, h)),
            pl.BlockSpec((None, S, MLA_HEAD_PAD), lambda b, h, i: (b, 0, h)),
            pl.BlockSpec((None, nk, MLA_V_DIM, tk), lambda b, h, i: (b, 0, h, 0)),
        ],
        out_specs=pl.BlockSpec((None, MLA_V_DIM, tq), lambda b, h, i: (b, h, i)),
        out_shape=jax.ShapeDtypeStruct((B, MLA_OUT_W, S), F32),
        compiler_params=pltpu.CompilerParams(
            dimension_semantics=("parallel", "parallel", "parallel"), vmem_limit_bytes=VMEM_LIMIT),
        name="mla",
    )(qm, km, vt)


def _post_kernel(x_ref, os_ref, omt_ref, p_ref, gswa_ref, gmla_ref, woa_ref, wob_ref, gffn_ref,
                 wg_ref, wu_ref, wd_ref, gple_ref, wpg_ref, wpp_ref, gfin_ref, y_ref):
    a = _rms(os_ref[...], gswa_ref[...]).astype(BF16)
    omt = omt_ref[...]
    r = lax.rsqrt(jnp.mean(omt * omt, axis=0, keepdims=True) + EPS)
    b = ((omt * r) * gmla_ref[...]).T.astype(BF16)
    mix = (jnp.dot(a, woa_ref[...], preferred_element_type=F32)
           + jnp.dot(b, wob_ref[...], preferred_element_type=F32))
    h = x_ref[...] + mix

    n2 = _rms(h, gffn_ref[...]).astype(BF16)
    gate = jnp.dot(n2, wg_ref[...], preferred_element_type=F32)
    up = jnp.dot(n2, wu_ref[...], preferred_element_type=F32)
    act = (gate * jax.nn.sigmoid(gate) * up).astype(BF16)
    h = h + jnp.dot(act, wd_ref[...], preferred_element_type=F32)

    n3 = _rms(h, gple_ref[...]).astype(BF16)
    pgate = jax.nn.sigmoid(jnp.dot(n3, wpg_ref[...], preferred_element_type=F32))
    emb = jnp.dot(p_ref[...].astype(BF16), wpp_ref[...], preferred_element_type=F32)
    h = h + pgate * emb
    y_ref[...] = _rms(h, gfin_ref[...])


def _post_call(x, o_swa, o_mla_t, p, w):
    B, S, _ = x.shape
    tm = POST_TILE
    const = lambda shape: pl.BlockSpec(shape, lambda b, i: (0,) * len(shape),
                                       pipeline_mode=pl.Buffered(1))
    tok = lambda width: pl.BlockSpec((None, tm, width), lambda b, i: (b, i, 0))
    half = SWA_Q_W
    return pl.pallas_call(
        _post_kernel,
        grid=(B, S // tm),
        in_specs=[
            tok(D_MODEL), tok(SWA_Q_W),
            pl.BlockSpec((None, MLA_OUT_W, tm), lambda b, i: (b, 0, i)),
            tok(PLE_DIM),
            const((1, half)), const((MLA_OUT_W, 1)), const((half, D_MODEL)), const((MLA_OUT_W, D_MODEL)),
            const((1, D_MODEL)), const((D_MODEL, D_FF)), const((D_MODEL, D_FF)), const((D_FF, D_MODEL)),
            const((1, D_MODEL)), const((D_MODEL, D_MODEL)), const((PLE_DIM, D_MODEL)),
            const((1, D_MODEL)),
        ],
        out_specs=tok(D_MODEL),
        out_shape=jax.ShapeDtypeStruct((B, S, D_MODEL), F32),
        compiler_params=pltpu.CompilerParams(
            dimension_semantics=("parallel", "parallel"), vmem_limit_bytes=VMEM_LIMIT),
        name="post",
    )(x, o_swa, o_mla_t, p, w["g_swa_out"], w["g_mla_out"], w["w_oa"], w["w_ob"], w["g_ffn"],
      w["w_ffn_gate"], w["w_ffn_up"], w["w_ffn_down"], w["g_ple"], w["w_ple_gate"],
      w["w_ple_proj"], w["g_final"])


def _rope_tables(S):
    half = MLA_ROPE_DIM // 2
    inv = ROPE_THETA ** (-jnp.arange(half, dtype=F32) / half)
    ang = jnp.arange(S, dtype=F32)[:, None] * inv[None, :]
    cos, sin = jnp.cos(ang), jnp.sin(ang)
    ones = jnp.ones((S, MLA_NOPE_DIM), F32)
    zeros = jnp.zeros((S, MLA_NOPE_DIM), F32)
    z16 = jnp.zeros((S, half), F32)
    pad = jnp.zeros((S, LANES - MLA_NOPE_DIM - MLA_ROPE_DIM), F32)
    cos_t = jnp.concatenate([ones, cos, cos, pad], axis=1)
    sina_t = jnp.concatenate([zeros, -sin, z16, pad], axis=1)
    sinb_t = jnp.concatenate([zeros, z16, sin, pad], axis=1)
    return cos_t, sina_t, sinb_t


def _prep_weights(g_mix, w_in, g_cq, w_uq, g_ckv, w_ukv, sink, g_swa_out, g_mla_out, w_o, g_ffn,
                  w_ffn_gate, w_ffn_up, w_ffn_down, g_ple, w_ple_gate, w_ple_proj, g_final):
    row = lambda g: g.reshape(1, -1).astype(F32)
    w_in0 = w_in[0]
    kr_cols = w_in0[:, IN_W - MLA_ROPE_DIM:]
    w_in_p = jnp.concatenate([
        w_in0[:, :IN_W - MLA_ROPE_DIM],
        jnp.zeros((D_MODEL, MLA_NOPE_DIM), F32), kr_cols,
        jnp.zeros((D_MODEL, LANES - MLA_NOPE_DIM - MLA_ROPE_DIM), F32)], axis=1)
    qd = MLA_NOPE_DIM + MLA_ROPE_DIM
    w_uq_p = jnp.pad(w_uq[0].reshape(MLA_Q_RANK, MLA_HEADS, qd),
                     ((0, 0), (0, 0), (0, MLA_HEAD_PAD - qd))).reshape(MLA_Q_RANK, MLA_PAD_W)
    w_ukv3 = w_ukv[0].reshape(MLA_KV_RANK, MLA_HEADS, MLA_NOPE_DIM + MLA_V_DIM)
    w_uk_p = jnp.pad(w_ukv3[:, :, :MLA_NOPE_DIM],
                     ((0, 0), (0, 0), (0, MLA_HEAD_PAD - MLA_NOPE_DIM))).reshape(MLA_KV_RANK, MLA_PAD_W)
    w_uvt = w_ukv3[:, :, MLA_NOPE_DIM:].reshape(MLA_KV_RANK, MLA_OUT_W).T
    return {
        "g_mix": row(g_mix[0]), "w_in": w_in_p.astype(BF16),
        "g_cq": row(g_cq[0]), "w_uq": w_uq_p.astype(BF16),
        "g_ckv": row(g_ckv[0]), "w_uk": w_uk_p.astype(BF16), "w_uvt": w_uvt.astype(BF16),
        "sink": sink[0].astype(F32),
        "g_swa_out": row(g_swa_out[0]), "g_mla_out": g_mla_out[0].reshape(-1, 1).astype(F32),
        "w_oa": w_o[0, :SWA_Q_W].astype(BF16), "w_ob": w_o[0, SWA_Q_W:].astype(BF16),
        "g_ffn": row(g_ffn[0]),
        "w_ffn_gate": w_ffn_gate[0].astype(BF16), "w_ffn_up": w_ffn_up[0].astype(BF16),
        "w_ffn_down": w_ffn_down[0].astype(BF16),
        "g_ple": row(g_ple[0]), "w_ple_gate": w_ple_gate[0].astype(BF16),
        "w_ple_proj": w_ple_proj[0].astype(BF16), "g_final": row(g_final),
    }


def _encoder(x, p, w):
    S = x.shape[1]
    cos, sina, sinb = _rope_tables(S)
    qs, ks, vs, qm, km, vt = _proj_call(x, cos, sina, sinb, w)
    o_swa = _swa_call(qs, ks, vs, w["sink"])
    o_mla_t = _mla_call(qm, km, vt)
    return _post_call(x, o_swa, o_mla_t, p[0], w)


def kernel(x_prompt, x_sample, p_prompt, p_sample, g_mix, w_in, g_cq, w_uq, g_ckv, w_ukv, sink,
           g_swa_out, g_mla_out, w_o, g_ffn, w_ffn_gate, w_ffn_up, w_ffn_down, g_ple, w_ple_gate,
           w_ple_proj, g_final):
    w = _prep_weights(g_mix, w_in, g_cq, w_uq, g_ckv, w_ukv, sink, g_swa_out, g_mla_out, w_o, g_ffn,
                      w_ffn_gate, w_ffn_up, w_ffn_down, g_ple, w_ple_gate, w_ple_proj, g_final)
    return (_encoder(x_prompt, p_prompt, w), _encoder(x_sample, p_sample, w))
```

```python
import functools
import math

import jax
import jax.numpy as jnp
import numpy as np
from jax import lax
from jax.experimental import pallas as pl
from jax.experimental.pallas import tpu as pltpu

F32 = jnp.float32
BF16 = jnp.bfloat16

D_MODEL = 1024
PLE_DIM = 256
WINDOW = 128
SWA_HEADS = 8
SWA_KV_HEADS = 2
SWA_GROUP = SWA_HEADS // SWA_KV_HEADS
SWA_HEAD_DIM = 64
MLA_HEADS = 8
MLA_NOPE_DIM = 64
MLA_ROPE_DIM = 32
MLA_V_DIM = 64
MLA_Q_RANK = 256
MLA_KV_RANK = 128
ROPE_THETA = 10000.0
D_FF = 2816
EPS = 1e-6
NEG_INF = -1e30

SWA_Q_W = SWA_HEADS * SWA_HEAD_DIM
SWA_KV_W = SWA_KV_HEADS * SWA_HEAD_DIM
MLA_OUT_W = MLA_HEADS * MLA_V_DIM
IN_W = SWA_Q_W + 2 * SWA_KV_W + MLA_Q_RANK + MLA_KV_RANK + MLA_ROPE_DIM

LANES = 128
MLA_HEAD_PAD = LANES
MLA_PAD_W = MLA_HEADS * MLA_HEAD_PAD
IN_W_PAD = 1280
OFF_CQ = SWA_Q_W + 2 * SWA_KV_W
OFF_CKV = OFF_CQ + MLA_Q_RANK
OFF_KR_BLK = OFF_CKV + MLA_KV_RANK

TOK_TILE = 512
MLA_TQ = 256
MLA_GROUP = 4
MLA_SUM_ROWS = 16
MLA_KV_COPY_PIECES = 8
SWA_TQ = 256
POST_TILE = 256
VMEM_LIMIT = 56 * 1024 * 1024

LOG2E = math.log2(math.e)
MLA_QSCALE = (MLA_NOPE_DIM + MLA_ROPE_DIM) ** -0.5 * LOG2E
SWA_QSCALE = SWA_HEAD_DIM ** -0.5


def _rms(x, g):
    r = lax.rsqrt(jnp.mean(x * x, axis=-1, keepdims=True) + EPS)
    return (x * r) * g


def _proj_kernel(x_ref, gmix_ref, win_ref, gcq_ref, wuq_ref, gckv_ref, wuk_ref, wuvt_ref,
                 cos_ref, sina_ref, sinb_ref,
                 qs_ref, ks_ref, vs_ref, qm_ref, km_ref, vt_ref):
    n = _rms(x_ref[...], gmix_ref[...]).astype(BF16)
    proj = jnp.dot(n, win_ref[...], preferred_element_type=F32)
    qs_ref[...] = (proj[:, :SWA_Q_W] * SWA_QSCALE).astype(BF16)
    ks_ref[...] = proj[:, SWA_Q_W:SWA_Q_W + SWA_KV_W].astype(BF16)
    vs_ref[...] = proj[:, SWA_Q_W + SWA_KV_W:OFF_CQ].astype(BF16)

    cq = _rms(proj[:, OFF_CQ:OFF_CKV], gcq_ref[...]).astype(BF16)
    qm = jnp.dot(cq, wuq_ref[...], preferred_element_type=F32)
    ckv = _rms(proj[:, OFF_CKV:OFF_KR_BLK], gckv_ref[...]).astype(BF16)
    kn = jnp.dot(ckv, wuk_ref[...], preferred_element_type=F32)
    vt = lax.dot_general(wuvt_ref[...], ckv, (((1,), (1,)), ((), ())),
                         preferred_element_type=F32)
    vt_ref[...] = vt.astype(BF16)

    cos, sina, sinb = cos_ref[...], sina_ref[...], sinb_ref[...]

    def rope(t):
        return (t * cos + pltpu.roll(t, LANES - MLA_ROPE_DIM // 2, 1) * sina
                + pltpu.roll(t, MLA_ROPE_DIM // 2, 1) * sinb)

    kr = rope(proj[:, OFF_KR_BLK:IN_W_PAD])
    for h in range(MLA_HEADS):
        sl = slice(h * MLA_HEAD_PAD, (h + 1) * MLA_HEAD_PAD)
        qm_ref[:, sl] = (rope(qm[:, sl]) * MLA_QSCALE).astype(BF16)
        km_ref[:, sl] = (kn[:, sl] + kr).astype(BF16)


def _proj_call(x, cos, sina, sinb, w):
    B, S, _ = x.shape
    tm = TOK_TILE
    nt = S // tm
    const = lambda shape: pl.BlockSpec(shape, lambda b, i: (0,) * len(shape))
    tok = lambda width: pl.BlockSpec((None, tm, width), lambda b, i: (b, i, 0))
    tab = pl.BlockSpec((tm, LANES), lambda b, i: (i, 0))
    out_shape = (
        jax.ShapeDtypeStruct((B, S, SWA_Q_W), BF16),
        jax.ShapeDtypeStruct((B, S, SWA_KV_W), BF16),
        jax.ShapeDtypeStruct((B, S, SWA_KV_W), BF16),
        jax.ShapeDtypeStruct((B, S, MLA_PAD_W), BF16),
        jax.ShapeDtypeStruct((B, S, MLA_PAD_W), BF16),
        jax.ShapeDtypeStruct((B, nt, MLA_OUT_W, tm), BF16),
    )
    return pl.pallas_call(
        _proj_kernel,
        grid=(B, nt),
        in_specs=[
            tok(D_MODEL), const((1, D_MODEL)), const((D_MODEL, IN_W_PAD)),
            const((1, MLA_Q_RANK)), const((MLA_Q_RANK, MLA_PAD_W)),
            const((1, MLA_KV_RANK)), const((MLA_KV_RANK, MLA_PAD_W)), const((MLA_OUT_W, MLA_KV_RANK)),
            tab, tab, tab,
        ],
        out_specs=(
            tok(SWA_Q_W), tok(SWA_KV_W), tok(SWA_KV_W), tok(MLA_PAD_W), tok(MLA_PAD_W),
            pl.BlockSpec((None, None, MLA_OUT_W, tm), lambda b, i: (b, i, 0, 0)),
        ),
        out_shape=out_shape,
        compiler_params=pltpu.CompilerParams(
            dimension_semantics=("parallel", "parallel"), vmem_limit_bytes=VMEM_LIMIT),
        name="proj",
    )(x, w["g_mix"], w["w_in"], w["g_cq"], w["w_uq"], w["g_ckv"], w["w_uk"], w["w_uvt"],
      cos, sina, sinb)


def _swa_kernel(sink_ref, q_ref, kp_ref, kc_ref, kn_ref, vp_ref, vc_ref, vn_ref, bias_ref, o_ref):
    i = pl.program_id(1)
    last = pl.num_programs(1) - 1
    kw = jnp.concatenate([kp_ref[...], kc_ref[...], kn_ref[...]], axis=0)
    vw = jnp.concatenate([vp_ref[...], vc_ref[...], vn_ref[...]], axis=0)
    nkeys = kw.shape[0]
    col = lax.broadcasted_iota(jnp.int32, (1, nkeys), 1)
    col_ok = ((col >= WINDOW) | (i > 0)) & ((col < nkeys - WINDOW) | (i < last))
    outs = []
    for h in range(SWA_HEADS):
        g = h // SWA_GROUP
        qh = q_ref[:, h * SWA_HEAD_DIM:(h + 1) * SWA_HEAD_DIM]
        kh = kw[:, g * SWA_HEAD_DIM:(g + 1) * SWA_HEAD_DIM]
        vh = vw[:, g * SWA_HEAD_DIM:(g + 1) * SWA_HEAD_DIM]
        s = lax.dot_general(qh, kh, (((1,), (1,)), ((), ())), preferred_element_type=F32)
        s = jnp.where(col_ok, s + bias_ref[h], NEG_INF)
        sink = sink_ref[h]
        m = jnp.maximum(jnp.max(s, axis=-1, keepdims=True), sink)
        e = jnp.exp(s - m)
        den = jnp.sum(e, axis=-1, keepdims=True) + jnp.exp(sink - m)
        o = jnp.dot(e.astype(BF16), vh, preferred_element_type=F32)
        outs.append(o / den)
    o_ref[...] = jnp.concatenate(outs, axis=-1)


def _swa_bias():
    slopes = 2.0 ** (-8.0 * np.arange(1, SWA_HEADS + 1, dtype=np.float32) / SWA_HEADS)
    a = np.arange(SWA_TQ)[:, None]
    c = np.arange(SWA_TQ + 2 * WINDOW)[None, :]
    dist = np.abs(c - WINDOW - a).astype(np.float32)
    bias = np.where(dist[None] <= WINDOW, -slopes[:, None, None] * dist[None], np.float32(NEG_INF))
    return jnp.asarray(bias, dtype=F32)


def _swa_call(qs, ks, vs, sink):
    B, S, _ = qs.shape
    tq = SWA_TQ
    nq = S // tq
    r = tq // WINDOW
    nhalo = S // WINDOW
    cur = lambda width: pl.BlockSpec((None, tq, width), lambda b, i: (b, i, 0))
    prev = pl.BlockSpec((None, WINDOW, SWA_KV_W), lambda b, i: (b, jnp.maximum(i * r - 1, 0), 0))
    nxt = pl.BlockSpec((None, WINDOW, SWA_KV_W),
                       lambda b, i: (b, jnp.minimum((i + 1) * r, nhalo - 1), 0))
    bias = _swa_bias()
    return pl.pallas_call(
        _swa_kernel,
        grid=(B, nq),
        in_specs=[
            pl.BlockSpec(memory_space=pltpu.SMEM),
            cur(SWA_Q_W), prev, cur(SWA_KV_W), nxt, prev, cur(SWA_KV_W), nxt,
            pl.BlockSpec(bias.shape, lambda b, i: (0, 0, 0)),
        ],
        out_specs=cur(SWA_Q_W),
        out_shape=jax.ShapeDtypeStruct((B, S, SWA_Q_W), F32),
        compiler_params=pltpu.CompilerParams(
            dimension_semantics=("parallel", "parallel"), vmem_limit_bytes=VMEM_LIMIT),
        name="swa",
    )(sink, qs, ks, ks, ks, vs, vs, vs, bias)


def _mla_kv_copies(k_hbm, vt_hbm, k_ref, vt_ref, sem):
    b, h = pl.program_id(0), pl.program_id(1)
    G = MLA_GROUP
    nk, _, tk = vt_ref.shape
    npiece = MLA_KV_COPY_PIECES
    cpp = nk // npiece
    copies = []
    for j in range(npiece):
        rows = pl.ds(j * cpp * tk, cpp * tk)
        copies.append(pltpu.make_async_copy(
            k_hbm.at[b, rows, pl.ds(h * (G * MLA_HEAD_PAD), G * MLA_HEAD_PAD)],
            k_ref.at[rows], sem.at[0, j]))
        chunks = pl.ds(j * cpp, cpp)
        copies.append(pltpu.make_async_copy(
            vt_hbm.at[b, chunks, pl.ds(h * (G * MLA_V_DIM), G * MLA_V_DIM), :],
            vt_ref.at[chunks], sem.at[1, j]))
    return copies


def _mla_kernel(q_ref, k_hbm, vt_hbm, o_ref, k_ref, vt_ref, sem, s_scr, p_scr, acc_scr):
    @pl.when(pl.program_id(2) == 0)
    def _():
        copies = _mla_kv_copies(k_hbm, vt_hbm, k_ref, vt_ref, sem)
        for cp in copies:
            cp.start()
        for cp in copies:
            cp.wait()

    nk, _, tk = vt_ref.shape
    tq = q_ref.shape[0]
    G = MLA_GROUP
    qs = [q_ref[:, g * MLA_HEAD_PAD:(g + 1) * MLA_HEAD_PAD] for g in range(G)]

    def scores(c, slot):
        rows = pl.ds(pl.multiple_of(c * tk, tk), tk)
        cms = []
        for g in range(G):
            k = k_ref[rows, g * MLA_HEAD_PAD:(g + 1) * MLA_HEAD_PAD]
            s = lax.dot_general(k, qs[g], (((1,), (1,)), ((), ())), preferred_element_type=F32)
            s_scr[slot, g] = s
            cms.append(jnp.max(s, axis=0, keepdims=True))
        return tuple(cms)

    def softmax(slot, cm, m):
        m_out, alphas = [], []
        for g in range(G):
            m_new = jnp.maximum(m[g], cm[g])
            p_scr[slot, g] = jnp.exp2(s_scr[slot, g] - m_new).astype(BF16)
            m_out.append(m_new)
            alphas.append(jnp.exp2(m[g] - m_new))
        return tuple(m_out), tuple(alphas)

    ones = jnp.ones((MLA_SUM_ROWS, tk), BF16)

    def values(c, slot, alpha):
        for g in range(G):
            vt = jnp.concatenate([vt_ref[c, g * MLA_V_DIM:(g + 1) * MLA_V_DIM, :], ones], axis=0)
            pv = jnp.dot(vt, p_scr[slot, g], preferred_element_type=F32)
            acc_scr[g] = alpha[g] * acc_scr[g] + pv

    def step(c, cur, carry, *, with_scores=True):
        cm, m, alpha_prev = carry
        nxt = 1 - cur
        cm_next = scores(c + 1, nxt) if with_scores else cm
        m, alpha = softmax(cur, cm, m)
        values(c - 1, nxt, alpha_prev)
        return cm_next, m, alpha

    acc_scr[...] = jnp.zeros_like(acc_scr)
    cm = scores(0, 0)
    m0 = tuple(jnp.full((1, tq), NEG_INF, F32) for _ in range(G))
    cm_next = scores(1, 1)
    m, alpha = softmax(0, cm, m0)
    carry = (cm_next, m, alpha)

    def pair(i, carry):
        c = 2 * i + 1
        carry = step(c, 1, carry)
        return step(c + 1, 0, carry)

    carry = lax.fori_loop(0, (nk - 2) // 2, pair, carry)
    _, _, alpha = step(nk - 1, 1, carry, with_scores=False)
    values(nk - 1, 1, alpha)
    for g in range(G):
        acc = acc_scr[g]
        o_ref[g * MLA_V_DIM:(g + 1) * MLA_V_DIM, :] = (
            acc[:MLA_V_DIM] / acc[MLA_V_DIM:MLA_V_DIM + 1])


def _mla_call(qm, km, vt):
    B, S, _ = qm.shape
    tq, G = MLA_TQ, MLA_GROUP
    nk, tk = vt.shape[1], vt.shape[3]
    assert nk % 2 == 0 and nk % MLA_KV_COPY_PIECES == 0
    return pl.pallas_call(
        _mla_kernel,
        grid=(B, MLA_HEADS // G, S // tq),
        in_specs=[
            pl.BlockSpec((None, tq, G * MLA_HEAD_PAD), lambda b, h, i: (b, i, h)),
            pl.BlockSpec(memory_space=pl.ANY),
            pl.BlockSpec(memory_space=pl.ANY),
        ],
        out_specs=pl.BlockSpec((None, G * MLA_V_DIM, tq), lambda b, h, i: (b, h, i)),
        out_shape=jax.ShapeDtypeStruct((B, MLA_OUT_W, S), F32),
        scratch_shapes=[
            pltpu.VMEM((S, G * MLA_HEAD_PAD), BF16),
            pltpu.VMEM((nk, G * MLA_V_DIM, tk), BF16),
            pltpu.SemaphoreType.DMA((2, MLA_KV_COPY_PIECES)),
            pltpu.VMEM((2, G, tk, tq), F32),
            pltpu.VMEM((2, G, tk, tq), BF16),
            pltpu.VMEM((G, MLA_V_DIM + MLA_SUM_ROWS, tq), F32),
        ],
        compiler_params=pltpu.CompilerParams(
            dimension_semantics=("arbitrary", "arbitrary", "arbitrary"),
            vmem_limit_bytes=VMEM_LIMIT),
        name="mla",
    )(qm, km, vt)


def _post_kernel(x_ref, os_ref, omt_ref, p_ref, gswa_ref, gmla_ref, woa_ref, wob_ref, gffn_ref,
                 wg_ref, wu_ref, wd_ref, gple_ref, wpg_ref, wpp_ref, gfin_ref, y_ref):
    a = _rms(os_ref[...], gswa_ref[...]).astype(BF16)
    omt = omt_ref[...]
    r = lax.rsqrt(jnp.mean(omt * omt, axis=0, keepdims=True) + EPS)
    b = ((omt * r) * gmla_ref[...]).T.astype(BF16)
    mix = (jnp.dot(a, woa_ref[...], preferred_element_type=F32)
           + jnp.dot(b, wob_ref[...], preferred_element_type=F32))
    h = x_ref[...] + mix

    n2 = _rms(h, gffn_ref[...]).astype(BF16)
    gate = jnp.dot(n2, wg_ref[...], preferred_element_type=F32)
    up = jnp.dot(n2, wu_ref[...], preferred_element_type=F32)
    act = (gate * jax.nn.sigmoid(gate) * up).astype(BF16)
    h = h + jnp.dot(act, wd_ref[...], preferred_element_type=F32)

    n3 = _rms(h, gple_ref[...]).astype(BF16)
    pgate = jax.nn.sigmoid(jnp.dot(n3, wpg_ref[...], preferred_element_type=F32))
    emb = jnp.dot(p_ref[...].astype(BF16), wpp_ref[...], preferred_element_type=F32)
    h = h + pgate * emb
    y_ref[...] = _rms(h, gfin_ref[...])


def _post_call(x, o_swa, o_mla_t, p, w):
    B, S, _ = x.shape
    tm = POST_TILE
    const = lambda shape: pl.BlockSpec(shape, lambda b, i: (0,) * len(shape),
                                       pipeline_mode=pl.Buffered(1))
    tok = lambda width: pl.BlockSpec((None, tm, width), lambda b, i: (b, i, 0))
    half = SWA_Q_W
    return pl.pallas_call(
        _post_kernel,
        grid=(B, S // tm),
        in_specs=[
            tok(D_MODEL), tok(SWA_Q_W),
            pl.BlockSpec((None, MLA_OUT_W, tm), lambda b, i: (b, 0, i)),
            tok(PLE_DIM),
            const((1, half)), const((MLA_OUT_W, 1)), const((half, D_MODEL)), const((MLA_OUT_W, D_MODEL)),
            const((1, D_MODEL)), const((D_MODEL, D_FF)), const((D_MODEL, D_FF)), const((D_FF, D_MODEL)),
            const((1, D_MODEL)), const((D_MODEL, D_MODEL)), const((PLE_DIM, D_MODEL)),
            const((1, D_MODEL)),
        ],
        out_specs=tok(D_MODEL),
        out_shape=jax.ShapeDtypeStruct((B, S, D_MODEL), F32),
        compiler_params=pltpu.CompilerParams(
            dimension_semantics=("parallel", "parallel"), vmem_limit_bytes=VMEM_LIMIT),
        name="post",
    )(x, o_swa, o_mla_t, p, w["g_swa_out"], w["g_mla_out"], w["w_oa"], w["w_ob"], w["g_ffn"],
      w["w_ffn_gate"], w["w_ffn_up"], w["w_ffn_down"], w["g_ple"], w["w_ple_gate"],
      w["w_ple_proj"], w["g_final"])


def _rope_tables(S):
    half = MLA_ROPE_DIM // 2
    inv = ROPE_THETA ** (-jnp.arange(half, dtype=F32) / half)
    ang = jnp.arange(S, dtype=F32)[:, None] * inv[None, :]
    cos, sin = jnp.cos(ang), jnp.sin(ang)
    ones = jnp.ones((S, MLA_NOPE_DIM), F32)
    zeros = jnp.zeros((S, MLA_NOPE_DIM), F32)
    z16 = jnp.zeros((S, half), F32)
    pad = jnp.zeros((S, LANES - MLA_NOPE_DIM - MLA_ROPE_DIM), F32)
    cos_t = jnp.concatenate([ones, cos, cos, pad], axis=1)
    sina_t = jnp.concatenate([zeros, -sin, z16, pad], axis=1)
    sinb_t = jnp.concatenate([zeros, z16, sin, pad], axis=1)
    return cos_t, sina_t, sinb_t


def _prep_weights(g_mix, w_in, g_cq, w_uq, g_ckv, w_ukv, sink, g_swa_out, g_mla_out, w_o, g_ffn,
                  w_ffn_gate, w_ffn_up, w_ffn_down, g_ple, w_ple_gate, w_ple_proj, g_final):
    row = lambda g: g.reshape(1, -1).astype(F32)
    w_in0 = w_in[0]
    kr_cols = w_in0[:, IN_W - MLA_ROPE_DIM:]
    w_in_p = jnp.concatenate([
        w_in0[:, :IN_W - MLA_ROPE_DIM],
        jnp.zeros((D_MODEL, MLA_NOPE_DIM), F32), kr_cols,
        jnp.zeros((D_MODEL, LANES - MLA_NOPE_DIM - MLA_ROPE_DIM), F32)], axis=1)
    qd = MLA_NOPE_DIM + MLA_ROPE_DIM
    w_uq_p = jnp.pad(w_uq[0].reshape(MLA_Q_RANK, MLA_HEADS, qd),
                     ((0, 0), (0, 0), (0, MLA_HEAD_PAD - qd))).reshape(MLA_Q_RANK, MLA_PAD_W)
    w_ukv3 = w_ukv[0].reshape(MLA_KV_RANK, MLA_HEADS, MLA_NOPE_DIM + MLA_V_DIM)
    w_uk_p = jnp.pad(w_ukv3[:, :, :MLA_NOPE_DIM],
                     ((0, 0), (0, 0), (0, MLA_HEAD_PAD - MLA_NOPE_DIM))).reshape(MLA_KV_RANK, MLA_PAD_W)
    w_uvt = w_ukv3[:, :, MLA_NOPE_DIM:].reshape(MLA_KV_RANK, MLA_OUT_W).T
    return {
        "g_mix": row(g_mix[0]), "w_in": w_in_p.astype(BF16),
        "g_cq": row(g_cq[0]), "w_uq": w_uq_p.astype(BF16),
        "g_ckv": row(g_ckv[0]), "w_uk": w_uk_p.astype(BF16), "w_uvt": w_uvt.astype(BF16),
        "sink": sink[0].astype(F32),
        "g_swa_out": row(g_swa_out[0]), "g_mla_out": g_mla_out[0].reshape(-1, 1).astype(F32),
        "w_oa": w_o[0, :SWA_Q_W].astype(BF16), "w_ob": w_o[0, SWA_Q_W:].astype(BF16),
        "g_ffn": row(g_ffn[0]),
        "w_ffn_gate": w_ffn_gate[0].astype(BF16), "w_ffn_up": w_ffn_up[0].astype(BF16),
        "w_ffn_down": w_ffn_down[0].astype(BF16),
        "g_ple": row(g_ple[0]), "w_ple_gate": w_ple_gate[0].astype(BF16),
        "w_ple_proj": w_ple_proj[0].astype(BF16), "g_final": row(g_final),
    }


def _encoder(x, p, w):
    S = x.shape[1]
    cos, sina, sinb = _rope_tables(S)
    qs, ks, vs, qm, km, vt = _proj_call(x, cos, sina, sinb, w)
    o_swa = _swa_call(qs, ks, vs, w["sink"])
    o_mla_t = _mla_call(qm, km, vt)
    return _post_call(x, o_swa, o_mla_t, p[0], w)


def kernel(x_prompt, x_sample, p_prompt, p_sample, g_mix, w_in, g_cq, w_uq, g_ckv, w_ukv, sink,
           g_swa_out, g_mla_out, w_o, g_ffn, w_ffn_gate, w_ffn_up, w_ffn_down, g_ple, w_ple_gate,
           w_ple_proj, g_final):
    w = _prep_weights(g_mix, w_in, g_cq, w_uq, g_ckv, w_ukv, sink, g_swa_out, g_mla_out, w_o, g_ffn,
                      w_ffn_gate, w_ffn_up, w_ffn_down, g_ple, w_ple_gate, w_ple_proj, g_final)
    return (_encoder(x_prompt, p_prompt, w), _encoder(x_sample, p_sample, w))
```

```python
import functools
import math

import jax
import jax.numpy as jnp
import numpy as np
from jax import lax
from jax.experimental import pallas as pl
from jax.experimental.pallas import tpu as pltpu

F32 = jnp.float32
BF16 = jnp.bfloat16

D_MODEL = 1024
PLE_DIM = 256
WINDOW = 128
SWA_HEADS = 8
SWA_KV_HEADS = 2
SWA_GROUP = SWA_HEADS // SWA_KV_HEADS
SWA_HEAD_DIM = 64
MLA_HEADS = 8
MLA_NOPE_DIM = 64
MLA_ROPE_DIM = 32
MLA_V_DIM = 64
MLA_Q_RANK = 256
MLA_KV_RANK = 128
ROPE_THETA = 10000.0
D_FF = 2816
EPS = 1e-6
NEG_INF = -1e30

SWA_Q_W = SWA_HEADS * SWA_HEAD_DIM
SWA_KV_W = SWA_KV_HEADS * SWA_HEAD_DIM
MLA_OUT_W = MLA_HEADS * MLA_V_DIM
IN_W = SWA_Q_W + 2 * SWA_KV_W + MLA_Q_RANK + MLA_KV_RANK + MLA_ROPE_DIM

LANES = 128
MLA_HEAD_PAD = LANES
MLA_PAD_W = MLA_HEADS * MLA_HEAD_PAD
IN_W_PAD = 1280
OFF_CQ = SWA_Q_W + 2 * SWA_KV_W
OFF_CKV = OFF_CQ + MLA_Q_RANK
OFF_KR_BLK = OFF_CKV + MLA_KV_RANK

TOK_TILE = 512
MLA_TQ = 256
MLA_GROUP = 4
MLA_SUM_ROWS = 16
MLA_KV_COPY_PIECES = 8
MLA_STEP_UNROLL = 4
MLA_REF_SLACK = 40.0
SWA_TQ = 256
POST_TILE = 256
VMEM_LIMIT = 56 * 1024 * 1024

LOG2E = math.log2(math.e)
MLA_QSCALE = (MLA_NOPE_DIM + MLA_ROPE_DIM) ** -0.5 * LOG2E
SWA_QSCALE = SWA_HEAD_DIM ** -0.5


def _rms(x, g):
    r = lax.rsqrt(jnp.mean(x * x, axis=-1, keepdims=True) + EPS)
    return (x * r) * g


def _proj_kernel(x_ref, gmix_ref, win_ref, gcq_ref, wuq_ref, gckv_ref, wuk_ref, wuvt_ref,
                 cos_ref, sina_ref, sinb_ref,
                 qs_ref, ks_ref, vs_ref, qm_ref, km_ref, vt_ref):
    n = _rms(x_ref[...], gmix_ref[...]).astype(BF16)
    proj = jnp.dot(n, win_ref[...], preferred_element_type=F32)
    qs_ref[...] = (proj[:, :SWA_Q_W] * SWA_QSCALE).astype(BF16)
    ks_ref[...] = proj[:, SWA_Q_W:SWA_Q_W + SWA_KV_W].astype(BF16)
    vs_ref[...] = proj[:, SWA_Q_W + SWA_KV_W:OFF_CQ].astype(BF16)

    cq = _rms(proj[:, OFF_CQ:OFF_CKV], gcq_ref[...]).astype(BF16)
    qm = jnp.dot(cq, wuq_ref[...], preferred_element_type=F32)
    ckv = _rms(proj[:, OFF_CKV:OFF_KR_BLK], gckv_ref[...]).astype(BF16)
    kn = jnp.dot(ckv, wuk_ref[...], preferred_element_type=F32)
    vt = lax.dot_general(wuvt_ref[...], ckv, (((1,), (1,)), ((), ())),
                         preferred_element_type=F32)
    vt_ref[...] = vt.astype(BF16)

    cos, sina, sinb = cos_ref[...], sina_ref[...], sinb_ref[...]

    def rope(t):
        return (t * cos + pltpu.roll(t, LANES - MLA_ROPE_DIM // 2, 1) * sina
                + pltpu.roll(t, MLA_ROPE_DIM // 2, 1) * sinb)

    kr = rope(proj[:, OFF_KR_BLK:IN_W_PAD])
    for h in range(MLA_HEADS):
        sl = slice(h * MLA_HEAD_PAD, (h + 1) * MLA_HEAD_PAD)
        qm_ref[:, sl] = (rope(qm[:, sl]) * MLA_QSCALE).astype(BF16)
        km_ref[:, sl] = (kn[:, sl] + kr).astype(BF16)


def _proj_call(x, cos, sina, sinb, w):
    B, S, _ = x.shape
    tm = TOK_TILE
    nt = S // tm
    const = lambda shape: pl.BlockSpec(shape, lambda b, i: (0,) * len(shape))
    tok = lambda width: pl.BlockSpec((None, tm, width), lambda b, i: (b, i, 0))
    tab = pl.BlockSpec((tm, LANES), lambda b, i: (i, 0))
    out_shape = (
        jax.ShapeDtypeStruct((B, S, SWA_Q_W), BF16),
        jax.ShapeDtypeStruct((B, S, SWA_KV_W), BF16),
        jax.ShapeDtypeStruct((B, S, SWA_KV_W), BF16),
        jax.ShapeDtypeStruct((B, S, MLA_PAD_W), BF16),
        jax.ShapeDtypeStruct((B, S, MLA_PAD_W), BF16),
        jax.ShapeDtypeStruct((B, nt, MLA_OUT_W, tm), BF16),
    )
    return pl.pallas_call(
        _proj_kernel,
        grid=(B, nt),
        in_specs=[
            tok(D_MODEL), const((1, D_MODEL)), const((D_MODEL, IN_W_PAD)),
            const((1, MLA_Q_RANK)), const((MLA_Q_RANK, MLA_PAD_W)),
            const((1, MLA_KV_RANK)), const((MLA_KV_RANK, MLA_PAD_W)), const((MLA_OUT_W, MLA_KV_RANK)),
            tab, tab, tab,
        ],
        out_specs=(
            tok(SWA_Q_W), tok(SWA_KV_W), tok(SWA_KV_W), tok(MLA_PAD_W), tok(MLA_PAD_W),
            pl.BlockSpec((None, None, MLA_OUT_W, tm), lambda b, i: (b, i, 0, 0)),
        ),
        out_shape=out_shape,
        compiler_params=pltpu.CompilerParams(
            dimension_semantics=("parallel", "parallel"), vmem_limit_bytes=VMEM_LIMIT),
        name="proj",
    )(x, w["g_mix"], w["w_in"], w["g_cq"], w["w_uq"], w["g_ckv"], w["w_uk"], w["w_uvt"],
      cos, sina, sinb)


def _swa_kernel(sink_ref, q_ref, kp_ref, kc_ref, kn_ref, vp_ref, vc_ref, vn_ref, bias_ref, o_ref):
    i = pl.program_id(1)
    last = pl.num_programs(1) - 1
    kw = jnp.concatenate([kp_ref[...], kc_ref[...], kn_ref[...]], axis=0)
    vw = jnp.concatenate([vp_ref[...], vc_ref[...], vn_ref[...]], axis=0)
    nkeys = kw.shape[0]
    col = lax.broadcasted_iota(jnp.int32, (1, nkeys), 1)
    col_ok = ((col >= WINDOW) | (i > 0)) & ((col < nkeys - WINDOW) | (i < last))
    outs = []
    for h in range(SWA_HEADS):
        g = h // SWA_GROUP
        qh = q_ref[:, h * SWA_HEAD_DIM:(h + 1) * SWA_HEAD_DIM]
        kh = kw[:, g * SWA_HEAD_DIM:(g + 1) * SWA_HEAD_DIM]
        vh = vw[:, g * SWA_HEAD_DIM:(g + 1) * SWA_HEAD_DIM]
        s = lax.dot_general(qh, kh, (((1,), (1,)), ((), ())), preferred_element_type=F32)
        s = jnp.where(col_ok, s + bias_ref[h], NEG_INF)
        sink = sink_ref[h]
        m = jnp.maximum(jnp.max(s, axis=-1, keepdims=True), sink)
        e = jnp.exp(s - m)
        den = jnp.sum(e, axis=-1, keepdims=True) + jnp.exp(sink - m)
        o = jnp.dot(e.astype(BF16), vh, preferred_element_type=F32)
        outs.append(o / den)
    o_ref[...] = jnp.concatenate(outs, axis=-1)


def _swa_bias():
    slopes = 2.0 ** (-8.0 * np.arange(1, SWA_HEADS + 1, dtype=np.float32) / SWA_HEADS)
    a = np.arange(SWA_TQ)[:, None]
    c = np.arange(SWA_TQ + 2 * WINDOW)[None, :]
    dist = np.abs(c - WINDOW - a).astype(np.float32)
    bias = np.where(dist[None] <= WINDOW, -slopes[:, None, None] * dist[None], np.float32(NEG_INF))
    return jnp.asarray(bias, dtype=F32)


def _swa_call(qs, ks, vs, sink):
    B, S, _ = qs.shape
    tq = SWA_TQ
    nq = S // tq
    r = tq // WINDOW
    nhalo = S // WINDOW
    cur = lambda width: pl.BlockSpec((None, tq, width), lambda b, i: (b, i, 0))
    prev = pl.BlockSpec((None, WINDOW, SWA_KV_W), lambda b, i: (b, jnp.maximum(i * r - 1, 0), 0))
    nxt = pl.BlockSpec((None, WINDOW, SWA_KV_W),
                       lambda b, i: (b, jnp.minimum((i + 1) * r, nhalo - 1), 0))
    bias = _swa_bias()
    return pl.pallas_call(
        _swa_kernel,
        grid=(B, nq),
        in_specs=[
            pl.BlockSpec(memory_space=pltpu.SMEM),
            cur(SWA_Q_W), prev, cur(SWA_KV_W), nxt, prev, cur(SWA_KV_W), nxt,
            pl.BlockSpec(bias.shape, lambda b, i: (0, 0, 0)),
        ],
        out_specs=cur(SWA_Q_W),
        out_shape=jax.ShapeDtypeStruct((B, S, SWA_Q_W), F32),
        compiler_params=pltpu.CompilerParams(
            dimension_semantics=("parallel", "parallel"), vmem_limit_bytes=VMEM_LIMIT),
        name="swa",
    )(sink, qs, ks, ks, ks, vs, vs, vs, bias)


def _mla_kv_copies(k_hbm, vt_hbm, k_ref, vt_ref, sem):
    b, h = pl.program_id(0), pl.program_id(1)
    G = MLA_GROUP
    nk, _, tk = vt_ref.shape
    npiece = MLA_KV_COPY_PIECES
    cpp = nk // npiece
    copies = []
    for j in range(npiece):
        rows = pl.ds(j * cpp * tk, cpp * tk)
        copies.append(pltpu.make_async_copy(
            k_hbm.at[b, rows, pl.ds(h * (G * MLA_HEAD_PAD), G * MLA_HEAD_PAD)],
            k_ref.at[rows], sem.at[0, j]))
        chunks = pl.ds(j * cpp, cpp)
        copies.append(pltpu.make_async_copy(
            vt_hbm.at[b, chunks, pl.ds(h * (G * MLA_V_DIM), G * MLA_V_DIM), :],
            vt_ref.at[chunks], sem.at[1, j]))
    return copies


def _mla_kernel(q_ref, k_hbm, vt_hbm, o_ref, k_ref, vt_ref, sem, s_scr, p_scr, acc_scr):
    @pl.when(pl.program_id(2) == 0)
    def _():
        copies = _mla_kv_copies(k_hbm, vt_hbm, k_ref, vt_ref, sem)
        for cp in copies:
            cp.start()
        for cp in copies:
            cp.wait()

    nk, _, tk = vt_ref.shape
    tq = q_ref.shape[0]
    G = MLA_GROUP
    qs = [q_ref[:, g * MLA_HEAD_PAD:(g + 1) * MLA_HEAD_PAD] for g in range(G)]

    def scores(c, slot):
        rows = pl.ds(pl.multiple_of(c * tk, tk), tk)
        cms = []
        for g in range(G):
            k = k_ref[rows, g * MLA_HEAD_PAD:(g + 1) * MLA_HEAD_PAD]
            s = lax.dot_general(k, qs[g], (((1,), (1,)), ((), ())), preferred_element_type=F32)
            s_scr[slot, g] = s
            cms.append(jnp.max(s, axis=0, keepdims=True))
        return tuple(cms)

    def softmax(slot, cm, m):
        m_out, alphas = [], []
        for g in range(G):
            m_new = jnp.maximum(m[g], cm[g])
            p_scr[slot, g] = jnp.exp2(s_scr[slot, g] - m_new).astype(BF16)
            m_out.append(m_new)
            alphas.append(jnp.exp2(m[g] - m_new))
        return tuple(m_out), tuple(alphas)

    ones = jnp.ones((MLA_SUM_ROWS, tk), BF16)

    def values(c, slot, alpha):
        for g in range(G):
            vt = jnp.concatenate([vt_ref[c, g * MLA_V_DIM:(g + 1) * MLA_V_DIM, :], ones], axis=0)
            pv = jnp.dot(vt, p_scr[slot, g], preferred_element_type=F32)
            acc_scr[g] = alpha[g] * acc_scr[g] + pv

    def step(c, cur, carry, *, with_scores=True):
        cm, m, alpha_prev = carry
        nxt = 1 - cur
        cm_next = scores(c + 1, nxt) if with_scores else cm
        m, alpha = softmax(cur, cm, m)
        values(c - 1, nxt, alpha_prev)
        return cm_next, m, alpha

    def finish():
        for g in range(G):
            acc = acc_scr[g]
            o_ref[g * MLA_V_DIM:(g + 1) * MLA_V_DIM, :] = (
                acc[:MLA_V_DIM] / acc[MLA_V_DIM:MLA_V_DIM + 1])

    def robust():
        acc_scr[...] = jnp.zeros_like(acc_scr)
        cm = scores(0, 0)
        m0 = tuple(jnp.full((1, tq), NEG_INF, F32) for _ in range(G))
        cm_next = scores(1, 1)
        m, alpha = softmax(0, cm, m0)
        carry = (cm_next, m, alpha)

        def pair(i, carry):
            c = 2 * i + 1
            carry = step(c, 1, carry)
            return step(c + 1, 0, carry)

        carry = lax.fori_loop(0, (nk - 2) // 2, pair, carry)
        _, _, alpha = step(nk - 1, 1, carry, with_scores=False)
        values(nk - 1, 1, alpha)
        finish()

    def fast_step(c, cur, carry):
        ref, cm1, cm2, a_cur, over = carry
        nxt = 1 - cur
        start = (c + 1) * tk
        rows = pl.ds(start if isinstance(c, int) else pl.multiple_of(start, tk), tk)
        ref_n, cm_n, a_n, over_n = [], [], [], []
        for g in range(G):
            r = jnp.maximum(ref[g], cm2[g])
            k = k_ref[rows, g * MLA_HEAD_PAD:(g + 1) * MLA_HEAD_PAD]
            s = lax.dot_general(k, qs[g], (((1,), (1,)), ((), ())), preferred_element_type=F32)
            p_scr[nxt, g] = jnp.exp2(s - r).astype(BF16)
            cm = jnp.max(s, axis=0, keepdims=True)
            ref_n.append(r)
            cm_n.append(cm)
            a_n.append(jnp.exp2(ref[g] - r))
            over_n.append(jnp.maximum(over[g], cm - r))
        values(c, cur, a_cur)
        return tuple(ref_n), tuple(cm_n), cm1, tuple(a_n), tuple(over_n)

    def fast():
        acc_scr[...] = jnp.zeros_like(acc_scr)
        ref0 = []
        rows = pl.ds(0, tk)
        for g in range(G):
            k = k_ref[rows, g * MLA_HEAD_PAD:(g + 1) * MLA_HEAD_PAD]
            s = lax.dot_general(k, qs[g], (((1,), (1,)), ((), ())), preferred_element_type=F32)
            cm = jnp.max(s, axis=0, keepdims=True)
            p_scr[0, g] = jnp.exp2(s - cm).astype(BF16)
            ref0.append(cm)
        ref0 = tuple(ref0)
        lowest = tuple(jnp.full((1, tq), NEG_INF, F32) for _ in range(G))
        one = tuple(jnp.ones((1, tq), F32) for _ in range(G))
        zero = tuple(jnp.zeros((1, tq), F32) for _ in range(G))
        carry = (ref0, ref0, lowest, one, zero)

        unroll = MLA_STEP_UNROLL

        def body(i, carry):
            for u in range(unroll):
                carry = fast_step(unroll * i + u, u % 2, carry)
            return carry

        nbody = (nk - 1) // unroll if nk > 2 * unroll else 0
        carry = lax.fori_loop(0, nbody, body, carry)
        for c in range(nbody * unroll, nk - 1):
            carry = fast_step(c, c % 2, carry)
        _, _, _, a_last, over = carry
        values(nk - 1, (nk - 1) % 2, a_last)
        finish()
        worst = over[0]
        for g in range(1, G):
            worst = jnp.maximum(worst, over[g])
        return jnp.max(worst)

    worst = fast()

    @pl.when(worst > MLA_REF_SLACK)
    def _():
        robust()


def _mla_call(qm, km, vt):
    B, S, _ = qm.shape
    tq, G = MLA_TQ, MLA_GROUP
    nk, tk = vt.shape[1], vt.shape[3]
    assert nk % 2 == 0 and nk % MLA_KV_COPY_PIECES == 0
    return pl.pallas_call(
        _mla_kernel,
        grid=(B, MLA_HEADS // G, S // tq),
        in_specs=[
            pl.BlockSpec((None, tq, G * MLA_HEAD_PAD), lambda b, h, i: (b, i, h)),
            pl.BlockSpec(memory_space=pl.ANY),
            pl.BlockSpec(memory_space=pl.ANY),
        ],
        out_specs=pl.BlockSpec((None, G * MLA_V_DIM, tq), lambda b, h, i: (b, h, i)),
        out_shape=jax.ShapeDtypeStruct((B, MLA_OUT_W, S), F32),
        scratch_shapes=[
            pltpu.VMEM((S, G * MLA_HEAD_PAD), BF16),
            pltpu.VMEM((nk, G * MLA_V_DIM, tk), BF16),
            pltpu.SemaphoreType.DMA((2, MLA_KV_COPY_PIECES)),
            pltpu.VMEM((2, G, tk, tq), F32),
            pltpu.VMEM((2, G, tk, tq), BF16),
            pltpu.VMEM((G, MLA_V_DIM + MLA_SUM_ROWS, tq), F32),
        ],
        compiler_params=pltpu.CompilerParams(
            dimension_semantics=("arbitrary", "arbitrary", "arbitrary"),
            vmem_limit_bytes=VMEM_LIMIT),
        name="mla",
    )(qm, km, vt)


def _post_kernel(x_ref, os_ref, omt_ref, p_ref, gswa_ref, gmla_ref, woa_ref, wob_ref, gffn_ref,
                 wg_ref, wu_ref, wd_ref, gple_ref, wpg_ref, wpp_ref, gfin_ref, y_ref):
    a = _rms(os_ref[...], gswa_ref[...]).astype(BF16)
    omt = omt_ref[...]
    r = lax.rsqrt(jnp.mean(omt * omt, axis=0, keepdims=True) + EPS)
    b = ((omt * r) * gmla_ref[...]).T.astype(BF16)
    mix = (jnp.dot(a, woa_ref[...], preferred_element_type=F32)
           + jnp.dot(b, wob_ref[...], preferred_element_type=F32))
    h = x_ref[...] + mix

    n2 = _rms(h, gffn_ref[...]).astype(BF16)
    gate = jnp.dot(n2, wg_ref[...], preferred_element_type=F32)
    up = jnp.dot(n2, wu_ref[...], preferred_element_type=F32)
    act = (gate * jax.nn.sigmoid(gate) * up).astype(BF16)
    h = h + jnp.dot(act, wd_ref[...], preferred_element_type=F32)

    n3 = _rms(h, gple_ref[...]).astype(BF16)
    pgate = jax.nn.sigmoid(jnp.dot(n3, wpg_ref[...], preferred_element_type=F32))
    emb = jnp.dot(p_ref[...].astype(BF16), wpp_ref[...], preferred_element_type=F32)
    h = h + pgate * emb
    y_ref[...] = _rms(h, gfin_ref[...])


def _post_call(x, o_swa, o_mla_t, p, w):
    B, S, _ = x.shape
    tm = POST_TILE
    const = lambda shape: pl.BlockSpec(shape, lambda b, i: (0,) * len(shape),
                                       pipeline_mode=pl.Buffered(1))
    tok = lambda width: pl.BlockSpec((None, tm, width), lambda b, i: (b, i, 0))
    half = SWA_Q_W
    return pl.pallas_call(
        _post_kernel,
        grid=(B, S // tm),
        in_specs=[
            tok(D_MODEL), tok(SWA_Q_W),
            pl.BlockSpec((None, MLA_OUT_W, tm), lambda b, i: (b, 0, i)),
            tok(PLE_DIM),
            const((1, half)), const((MLA_OUT_W, 1)), const((half, D_MODEL)), const((MLA_OUT_W, D_MODEL)),
            const((1, D_MODEL)), const((D_MODEL, D_FF)), const((D_MODEL, D_FF)), const((D_FF, D_MODEL)),
            const((1, D_MODEL)), const((D_MODEL, D_MODEL)), const((PLE_DIM, D_MODEL)),
            const((1, D_MODEL)),
        ],
        out_specs=tok(D_MODEL),
        out_shape=jax.ShapeDtypeStruct((B, S, D_MODEL), F32),
        compiler_params=pltpu.CompilerParams(
            dimension_semantics=("parallel", "parallel"), vmem_limit_bytes=VMEM_LIMIT),
        name="post",
    )(x, o_swa, o_mla_t, p, w["g_swa_out"], w["g_mla_out"], w["w_oa"], w["w_ob"], w["g_ffn"],
      w["w_ffn_gate"], w["w_ffn_up"], w["w_ffn_down"], w["g_ple"], w["w_ple_gate"],
      w["w_ple_proj"], w["g_final"])


def _rope_tables(S):
    half = MLA_ROPE_DIM // 2
    inv = ROPE_THETA ** (-jnp.arange(half, dtype=F32) / half)
    ang = jnp.arange(S, dtype=F32)[:, None] * inv[None, :]
    cos, sin = jnp.cos(ang), jnp.sin(ang)
    ones = jnp.ones((S, MLA_NOPE_DIM), F32)
    zeros = jnp.zeros((S, MLA_NOPE_DIM), F32)
    z16 = jnp.zeros((S, half), F32)
    pad = jnp.zeros((S, LANES - MLA_NOPE_DIM - MLA_ROPE_DIM), F32)
    cos_t = jnp.concatenate([ones, cos, cos, pad], axis=1)
    sina_t = jnp.concatenate([zeros, -sin, z16, pad], axis=1)
    sinb_t = jnp.concatenate([zeros, z16, sin, pad], axis=1)
    return cos_t, sina_t, sinb_t


def _prep_weights(g_mix, w_in, g_cq, w_uq, g_ckv, w_ukv, sink, g_swa_out, g_mla_out, w_o, g_ffn,
                  w_ffn_gate, w_ffn_up, w_ffn_down, g_ple, w_ple_gate, w_ple_proj, g_final):
    row = lambda g: g.reshape(1, -1).astype(F32)
    w_in0 = w_in[0]
    kr_cols = w_in0[:, IN_W - MLA_ROPE_DIM:]
    w_in_p = jnp.concatenate([
        w_in0[:, :IN_W - MLA_ROPE_DIM],
        jnp.zeros((D_MODEL, MLA_NOPE_DIM), F32), kr_cols,
        jnp.zeros((D_MODEL, LANES - MLA_NOPE_DIM - MLA_ROPE_DIM), F32)], axis=1)
    qd = MLA_NOPE_DIM + MLA_ROPE_DIM
    w_uq_p = jnp.pad(w_uq[0].reshape(MLA_Q_RANK, MLA_HEADS, qd),
                     ((0, 0), (0, 0), (0, MLA_HEAD_PAD - qd))).reshape(MLA_Q_RANK, MLA_PAD_W)
    w_ukv3 = w_ukv[0].reshape(MLA_KV_RANK, MLA_HEADS, MLA_NOPE_DIM + MLA_V_DIM)
    w_uk_p = jnp.pad(w_ukv3[:, :, :MLA_NOPE_DIM],
                     ((0, 0), (0, 0), (0, MLA_HEAD_PAD - MLA_NOPE_DIM))).reshape(MLA_KV_RANK, MLA_PAD_W)
    w_uvt = w_ukv3[:, :, MLA_NOPE_DIM:].reshape(MLA_KV_RANK, MLA_OUT_W).T
    return {
        "g_mix": row(g_mix[0]), "w_in": w_in_p.astype(BF16),
        "g_cq": row(g_cq[0]), "w_uq": w_uq_p.astype(BF16),
        "g_ckv": row(g_ckv[0]), "w_uk": w_uk_p.astype(BF16), "w_uvt": w_uvt.astype(BF16),
        "sink": sink[0].astype(F32),
        "g_swa_out": row(g_swa_out[0]), "g_mla_out": g_mla_out[0].reshape(-1, 1).astype(F32),
        "w_oa": w_o[0, :SWA_Q_W].astype(BF16), "w_ob": w_o[0, SWA_Q_W:].astype(BF16),
        "g_ffn": row(g_ffn[0]),
        "w_ffn_gate": w_ffn_gate[0].astype(BF16), "w_ffn_up": w_ffn_up[0].astype(BF16),
        "w_ffn_down": w_ffn_down[0].astype(BF16),
        "g_ple": row(g_ple[0]), "w_ple_gate": w_ple_gate[0].astype(BF16),
        "w_ple_proj": w_ple_proj[0].astype(BF16), "g_final": row(g_final),
    }


def _encoder(x, p, w):
    S = x.shape[1]
    cos, sina, sinb = _rope_tables(S)
    qs, ks, vs, qm, km, vt = _proj_call(x, cos, sina, sinb, w)
    o_swa = _swa_call(qs, ks, vs, w["sink"])
    o_mla_t = _mla_call(qm, km, vt)
    return _post_call(x, o_swa, o_mla_t, p[0], w)


def kernel(x_prompt, x_sample, p_prompt, p_sample, g_mix, w_in, g_cq, w_uq, g_ckv, w_ukv, sink,
           g_swa_out, g_mla_out, w_o, g_ffn, w_ffn_gate, w_ffn_up, w_ffn_down, g_ple, w_ple_gate,
           w_ple_proj, g_final):
    w = _prep_weights(g_mix, w_in, g_cq, w_uq, g_ckv, w_ukv, sink, g_swa_out, g_mla_out, w_o, g_ffn,
                      w_ffn_gate, w_ffn_up, w_ffn_down, g_ple, w_ple_gate, w_ple_proj, g_final)
    return (_encoder(x_prompt, p_prompt, w), _encoder(x_sample, p_sample, w))
```

```python
import functools
import math

import jax
import jax.numpy as jnp
import numpy as np
from jax import lax
from jax.experimental import pallas as pl
from jax.experimental.pallas import tpu as pltpu

F32 = jnp.float32
BF16 = jnp.bfloat16

D_MODEL = 1024
PLE_DIM = 256
WINDOW = 128
SWA_HEADS = 8
SWA_KV_HEADS = 2
SWA_GROUP = SWA_HEADS // SWA_KV_HEADS
SWA_HEAD_DIM = 64
MLA_HEADS = 8
MLA_NOPE_DIM = 64
MLA_ROPE_DIM = 32
MLA_V_DIM = 64
MLA_Q_RANK = 256
MLA_KV_RANK = 128
ROPE_THETA = 10000.0
D_FF = 2816
EPS = 1e-6
NEG_INF = -1e30

SWA_Q_W = SWA_HEADS * SWA_HEAD_DIM
SWA_KV_W = SWA_KV_HEADS * SWA_HEAD_DIM
MLA_OUT_W = MLA_HEADS * MLA_V_DIM
IN_W = SWA_Q_W + 2 * SWA_KV_W + MLA_Q_RANK + MLA_KV_RANK + MLA_ROPE_DIM

LANES = 128
MLA_HEAD_PAD = LANES
MLA_PAD_W = MLA_HEADS * MLA_HEAD_PAD
IN_W_PAD = 1280
OFF_CQ = SWA_Q_W + 2 * SWA_KV_W
OFF_CKV = OFF_CQ + MLA_Q_RANK
OFF_KR_BLK = OFF_CKV + MLA_KV_RANK

TOK_TILE = 512
MLA_TQ = 256
MLA_GROUP = 4
MLA_SUM_ROWS = 16
MLA_KV_COPY_PIECES = 8
MLA_STEP_UNROLL = 8
MLA_REF_SLACK = 40.0
SWA_TQ = 256
PROJ_SPLIT = 4
POST_TILE = 512
POST_SPLIT = 4
VMEM_LIMIT = 56 * 1024 * 1024

LOG2E = math.log2(math.e)
MLA_QSCALE = (MLA_NOPE_DIM + MLA_ROPE_DIM) ** -0.5 * LOG2E
SWA_QSCALE = SWA_HEAD_DIM ** -0.5 * LOG2E


def _rms(x, g):
    r = lax.rsqrt(jnp.mean(x * x, axis=-1, keepdims=True) + EPS)
    return (x * r) * g


def _proj_kernel(x_ref, gmix_ref, win_ref, gcq_ref, wuq_ref, gckv_ref, wuk_ref, wuvt_ref,
                 cos_ref, sina_ref, sinb_ref,
                 qs_ref, ks_ref, vs_ref, qm_ref, km_ref, vt_ref):
    rows = x_ref.shape[0] // PROJ_SPLIT
    groups = [slice(j * rows, (j + 1) * rows) for j in range(PROJ_SPLIT)]

    projs = []
    for sl in groups:
        n = _rms(x_ref[sl, :], gmix_ref[...]).astype(BF16)
        projs.append(jnp.dot(n, win_ref[...], preferred_element_type=F32))

    ups = []
    for sl, proj in zip(groups, projs):
        qs_ref[sl, :] = (proj[:, :SWA_Q_W] * SWA_QSCALE).astype(BF16)
        ks_ref[sl, :] = proj[:, SWA_Q_W:SWA_Q_W + SWA_KV_W].astype(BF16)
        vs_ref[:, sl] = proj[:, SWA_Q_W + SWA_KV_W:OFF_CQ].T.astype(BF16)
        cq = _rms(proj[:, OFF_CQ:OFF_CKV], gcq_ref[...]).astype(BF16)
        qm = jnp.dot(cq, wuq_ref[...], preferred_element_type=F32)
        ckv = _rms(proj[:, OFF_CKV:OFF_KR_BLK], gckv_ref[...]).astype(BF16)
        kn = jnp.dot(ckv, wuk_ref[...], preferred_element_type=F32)
        vt = lax.dot_general(wuvt_ref[...], ckv, (((1,), (1,)), ((), ())),
                             preferred_element_type=F32)
        vt_ref[:, sl] = vt.astype(BF16)
        ups.append((qm, kn))

    for sl, proj, (qm, kn) in zip(groups, projs, ups):
        cos, sina, sinb = cos_ref[sl, :], sina_ref[sl, :], sinb_ref[sl, :]

        def rope(t):
            return (t * cos + pltpu.roll(t, LANES - MLA_ROPE_DIM // 2, 1) * sina
                    + pltpu.roll(t, MLA_ROPE_DIM // 2, 1) * sinb)

        kr = rope(proj[:, OFF_KR_BLK:IN_W_PAD])
        for h in range(MLA_HEADS):
            hl = slice(h * MLA_HEAD_PAD, (h + 1) * MLA_HEAD_PAD)
            qm_ref[sl, hl] = (rope(qm[:, hl]) * MLA_QSCALE).astype(BF16)
            km_ref[sl, hl] = (kn[:, hl] + kr).astype(BF16)


def _proj_call(x, cos, sina, sinb, w):
    B, S, _ = x.shape
    tm = TOK_TILE
    nt = S // tm
    const = lambda shape: pl.BlockSpec(shape, lambda b, i: (0,) * len(shape))
    tok = lambda width: pl.BlockSpec((None, tm, width), lambda b, i: (b, i, 0))
    tab = pl.BlockSpec((tm, LANES), lambda b, i: (i, 0))
    out_shape = (
        jax.ShapeDtypeStruct((B, S, SWA_Q_W), BF16),
        jax.ShapeDtypeStruct((B, S, SWA_KV_W), BF16),
        jax.ShapeDtypeStruct((B, SWA_KV_W, S), BF16),
        jax.ShapeDtypeStruct((B, S, MLA_PAD_W), BF16),
        jax.ShapeDtypeStruct((B, S, MLA_PAD_W), BF16),
        jax.ShapeDtypeStruct((B, nt, MLA_OUT_W, tm), BF16),
    )
    return pl.pallas_call(
        _proj_kernel,
        grid=(B, nt),
        in_specs=[
            tok(D_MODEL), const((1, D_MODEL)), const((D_MODEL, IN_W_PAD)),
            const((1, MLA_Q_RANK)), const((MLA_Q_RANK, MLA_PAD_W)),
            const((1, MLA_KV_RANK)), const((MLA_KV_RANK, MLA_PAD_W)), const((MLA_OUT_W, MLA_KV_RANK)),
            tab, tab, tab,
        ],
        out_specs=(
            tok(SWA_Q_W), tok(SWA_KV_W),
            pl.BlockSpec((None, SWA_KV_W, tm), lambda b, i: (b, 0, i)),
            tok(MLA_PAD_W), tok(MLA_PAD_W),
            pl.BlockSpec((None, None, MLA_OUT_W, tm), lambda b, i: (b, i, 0, 0)),
        ),
        out_shape=out_shape,
        compiler_params=pltpu.CompilerParams(
            dimension_semantics=("parallel", "parallel"), vmem_limit_bytes=VMEM_LIMIT),
        name="proj",
    )(x, w["g_mix"], w["w_in"], w["g_cq"], w["w_uq"], w["g_ckv"], w["w_uk"], w["w_uvt"],
      cos, sina, sinb)


def _swa_kernel(sink_ref, q_ref, kp_ref, kc_ref, kn_ref, vp_ref, vc_ref, vn_ref, bias_ref, o_ref):
    i = pl.program_id(1)
    last = pl.num_programs(1) - 1
    kw = jnp.concatenate([kp_ref[...], kc_ref[...], kn_ref[...]], axis=0)
    vwt = jnp.concatenate([vp_ref[...], vc_ref[...], vn_ref[...]], axis=1)
    nkeys = kw.shape[0]
    row = lax.broadcasted_iota(jnp.int32, (nkeys, 1), 0)
    row_ok = ((row >= WINDOW) | (i > 0)) & ((row < nkeys - WINDOW) | (i < last))
    ones = jnp.ones((MLA_SUM_ROWS, nkeys), BF16)

    def logits(h):
        g = h // SWA_GROUP
        qh = q_ref[:, h * SWA_HEAD_DIM:(h + 1) * SWA_HEAD_DIM]
        kh = kw[:, g * SWA_HEAD_DIM:(g + 1) * SWA_HEAD_DIM]
        s = lax.dot_general(kh, qh, (((1,), (1,)), ((), ())), preferred_element_type=F32)
        return jnp.where(row_ok, s + bias_ref[h], NEG_INF)

    def attend(h, s):
        g = h // SWA_GROUP
        sink = sink_ref[h] * LOG2E
        m = jnp.maximum(jnp.max(s, axis=0, keepdims=True), sink)
        e = jnp.exp2(s - m).astype(BF16)
        vt = jnp.concatenate([vwt[g * SWA_HEAD_DIM:(g + 1) * SWA_HEAD_DIM, :], ones], axis=0)
        o = jnp.dot(vt, e, preferred_element_type=F32)
        den = o[SWA_HEAD_DIM:SWA_HEAD_DIM + 1] + jnp.exp2(sink - m)
        o_ref[h * SWA_HEAD_DIM:(h + 1) * SWA_HEAD_DIM, :] = o[:SWA_HEAD_DIM] / den

    ahead = 3
    pending = [logits(h) for h in range(ahead)]
    for h in range(SWA_HEADS):
        if h + ahead < SWA_HEADS:
            pending.append(logits(h + ahead))
        attend(h, pending.pop(0))


def _swa_bias():
    slopes = 2.0 ** (-8.0 * np.arange(1, SWA_HEADS + 1, dtype=np.float32) / SWA_HEADS)
    c = np.arange(SWA_TQ + 2 * WINDOW)[:, None]
    a = np.arange(SWA_TQ)[None, :]
    dist = np.abs(c - WINDOW - a).astype(np.float32)
    bias = np.where(dist[None] <= WINDOW, -slopes[:, None, None] * dist[None] * np.float32(LOG2E),
                    np.float32(NEG_INF))
    return jnp.asarray(bias, dtype=F32)


def _swa_call(qs, ks, vst, sink):
    B, S, _ = qs.shape
    tq = SWA_TQ
    nq = S // tq
    r = tq // WINDOW
    nhalo = S // WINDOW
    cur = lambda width: pl.BlockSpec((None, tq, width), lambda b, i: (b, i, 0))
    prev = pl.BlockSpec((None, WINDOW, SWA_KV_W), lambda b, i: (b, jnp.maximum(i * r - 1, 0), 0))
    nxt = pl.BlockSpec((None, WINDOW, SWA_KV_W),
                       lambda b, i: (b, jnp.minimum((i + 1) * r, nhalo - 1), 0))
    vprev = pl.BlockSpec((None, SWA_KV_W, WINDOW), lambda b, i: (b, 0, jnp.maximum(i * r - 1, 0)))
    vcur = pl.BlockSpec((None, SWA_KV_W, tq), lambda b, i: (b, 0, i))
    vnxt = pl.BlockSpec((None, SWA_KV_W, WINDOW),
                        lambda b, i: (b, 0, jnp.minimum((i + 1) * r, nhalo - 1)))
    bias = _swa_bias()
    return pl.pallas_call(
        _swa_kernel,
        grid=(B, nq),
        in_specs=[
            pl.BlockSpec(memory_space=pltpu.SMEM),
            cur(SWA_Q_W), prev, cur(SWA_KV_W), nxt, vprev, vcur, vnxt,
            pl.BlockSpec(bias.shape, lambda b, i: (0, 0, 0)),
        ],
        out_specs=pl.BlockSpec((None, SWA_Q_W, tq), lambda b, i: (b, 0, i)),
        out_shape=jax.ShapeDtypeStruct((B, SWA_Q_W, S), F32),
        compiler_params=pltpu.CompilerParams(
            dimension_semantics=("parallel", "parallel"), vmem_limit_bytes=VMEM_LIMIT),
        name="swa",
    )(sink, qs, ks, ks, ks, vst, vst, vst, bias)


def _mla_kv_copies(k_hbm, vt_hbm, k_ref, vt_ref, sem):
    b, h = pl.program_id(0), pl.program_id(1)
    G = MLA_GROUP
    nk, _, tk = vt_ref.shape
    npiece = MLA_KV_COPY_PIECES
    cpp = nk // npiece
    copies = []
    for j in range(npiece):
        rows = pl.ds(j * cpp * tk, cpp * tk)
        copies.append(pltpu.make_async_copy(
            k_hbm.at[b, rows, pl.ds(h * (G * MLA_HEAD_PAD), G * MLA_HEAD_PAD)],
            k_ref.at[rows], sem.at[0, j]))
        chunks = pl.ds(j * cpp, cpp)
        copies.append(pltpu.make_async_copy(
            vt_hbm.at[b, chunks, pl.ds(h * (G * MLA_V_DIM), G * MLA_V_DIM), :],
            vt_ref.at[chunks], sem.at[1, j]))
    return copies


def _mla_kernel(q_ref, k_hbm, vt_hbm, o_ref, k_ref, vt_ref, sem, s_scr, p_scr, acc_scr):
    @pl.when(pl.program_id(2) == 0)
    def _():
        copies = _mla_kv_copies(k_hbm, vt_hbm, k_ref, vt_ref, sem)
        for cp in copies:
            cp.start()
        for cp in copies:
            cp.wait()

    nk, _, tk = vt_ref.shape
    tq = q_ref.shape[0]
    G = MLA_GROUP
    qs = [q_ref[:, g * MLA_HEAD_PAD:(g + 1) * MLA_HEAD_PAD] for g in range(G)]

    def scores(c, slot):
        rows = pl.ds(pl.multiple_of(c * tk, tk), tk)
        cms = []
        for g in range(G):
            k = k_ref[rows, g * MLA_HEAD_PAD:(g + 1) * MLA_HEAD_PAD]
            s = lax.dot_general(k, qs[g], (((1,), (1,)), ((), ())), preferred_element_type=F32)
            s_scr[slot, g] = s
            cms.append(jnp.max(s, axis=0, keepdims=True))
        return tuple(cms)

    def softmax(slot, cm, m):
        m_out, alphas = [], []
        for g in range(G):
            m_new = jnp.maximum(m[g], cm[g])
            p_scr[slot, g] = jnp.exp2(s_scr[slot, g] - m_new).astype(BF16)
            m_out.append(m_new)
            alphas.append(jnp.exp2(m[g] - m_new))
        return tuple(m_out), tuple(alphas)

    ones = jnp.ones((MLA_SUM_ROWS, tk), BF16)

    def values(c, slot, alpha):
        for g in range(G):
            vt = jnp.concatenate([vt_ref[c, g * MLA_V_DIM:(g + 1) * MLA_V_DIM, :], ones], axis=0)
            pv = jnp.dot(vt, p_scr[slot, g], preferred_element_type=F32)
            acc_scr[g] = alpha[g] * acc_scr[g] + pv

    def step(c, cur, carry, *, with_scores=True):
        cm, m, alpha_prev = carry
        nxt = 1 - cur
        cm_next = scores(c + 1, nxt) if with_scores else cm
        m, alpha = softmax(cur, cm, m)
        values(c - 1, nxt, alpha_prev)
        return cm_next, m, alpha

    def finish():
        for g in range(G):
            acc = acc_scr[g]
            o_ref[g * MLA_V_DIM:(g + 1) * MLA_V_DIM, :] = (
                acc[:MLA_V_DIM] / acc[MLA_V_DIM:MLA_V_DIM + 1])

    def robust():
        acc_scr[...] = jnp.zeros_like(acc_scr)
        cm = scores(0, 0)
        m0 = tuple(jnp.full((1, tq), NEG_INF, F32) for _ in range(G))
        cm_next = scores(1, 1)
        m, alpha = softmax(0, cm, m0)
        carry = (cm_next, m, alpha)

        def pair(i, carry):
            c = 2 * i + 1
            carry = step(c, 1, carry)
            return step(c + 1, 0, carry)

        carry = lax.fori_loop(0, (nk - 2) // 2, pair, carry)
        _, _, alpha = step(nk - 1, 1, carry, with_scores=False)
        values(nk - 1, 1, alpha)
        finish()

    def fast_step(c, cur, carry):
        ref, cm1, cm2, a_cur, over = carry
        nxt = 1 - cur
        start = (c + 1) * tk
        rows = pl.ds(start if isinstance(c, int) else pl.multiple_of(start, tk), tk)
        ref_n, cm_n, a_n, over_n = [], [], [], []
        for g in range(G):
            r = jnp.maximum(ref[g], cm2[g])
            k = k_ref[rows, g * MLA_HEAD_PAD:(g + 1) * MLA_HEAD_PAD]
            s = lax.dot_general(k, qs[g], (((1,), (1,)), ((), ())), preferred_element_type=F32)
            p_scr[nxt, g] = jnp.exp2(s - r).astype(BF16)
            cm = jnp.max(s, axis=0, keepdims=True)
            ref_n.append(r)
            cm_n.append(cm)
            a_n.append(jnp.exp2(ref[g] - r))
            over_n.append(jnp.maximum(over[g], cm - r))
        values(c, cur, a_cur)
        return tuple(ref_n), tuple(cm_n), cm1, tuple(a_n), tuple(over_n)

    def fast():
        acc_scr[...] = jnp.zeros_like(acc_scr)
        ref0 = []
        rows = pl.ds(0, tk)
        for g in range(G):
            k = k_ref[rows, g * MLA_HEAD_PAD:(g + 1) * MLA_HEAD_PAD]
            s = lax.dot_general(k, qs[g], (((1,), (1,)), ((), ())), preferred_element_type=F32)
            cm = jnp.max(s, axis=0, keepdims=True)
            p_scr[0, g] = jnp.exp2(s - cm).astype(BF16)
            ref0.append(cm)
        ref0 = tuple(ref0)
        lowest = tuple(jnp.full((1, tq), NEG_INF, F32) for _ in range(G))
        one = tuple(jnp.ones((1, tq), F32) for _ in range(G))
        zero = tuple(jnp.zeros((1, tq), F32) for _ in range(G))
        carry = (ref0, ref0, lowest, one, zero)

        unroll = MLA_STEP_UNROLL

        def body(i, carry):
            for u in range(unroll):
                carry = fast_step(unroll * i + u, u % 2, carry)
            return carry

        nbody = (nk - 1) // unroll if nk > 2 * unroll else 0
        carry = lax.fori_loop(0, nbody, body, carry)
        for c in range(nbody * unroll, nk - 1):
            carry = fast_step(c, c % 2, carry)
        _, _, _, a_last, over = carry
        values(nk - 1, (nk - 1) % 2, a_last)
        finish()
        worst = over[0]
        for g in range(1, G):
            worst = jnp.maximum(worst, over[g])
        return jnp.max(worst)

    worst = fast()

    @pl.when(worst > MLA_REF_SLACK)
    def _():
        robust()


def _mla_call(qm, km, vt):
    B, S, _ = qm.shape
    tq, G = MLA_TQ, MLA_GROUP
    nk, tk = vt.shape[1], vt.shape[3]
    assert nk % 2 == 0 and nk % MLA_KV_COPY_PIECES == 0
    return pl.pallas_call(
        _mla_kernel,
        grid=(B, MLA_HEADS // G, S // tq),
        in_specs=[
            pl.BlockSpec((None, tq, G * MLA_HEAD_PAD), lambda b, h, i: (b, i, h)),
            pl.BlockSpec(memory_space=pl.ANY),
            pl.BlockSpec(memory_space=pl.ANY),
        ],
        out_specs=pl.BlockSpec((None, G * MLA_V_DIM, tq), lambda b, h, i: (b, h, i)),
        out_shape=jax.ShapeDtypeStruct((B, MLA_OUT_W, S), F32),
        scratch_shapes=[
            pltpu.VMEM((S, G * MLA_HEAD_PAD), BF16),
            pltpu.VMEM((nk, G * MLA_V_DIM, tk), BF16),
            pltpu.SemaphoreType.DMA((2, MLA_KV_COPY_PIECES)),
            pltpu.VMEM((2, G, tk, tq), F32),
            pltpu.VMEM((2, G, tk, tq), BF16),
            pltpu.VMEM((G, MLA_V_DIM + MLA_SUM_ROWS, tq), F32),
        ],
        compiler_params=pltpu.CompilerParams(
            dimension_semantics=("arbitrary", "arbitrary", "arbitrary"),
            vmem_limit_bytes=VMEM_LIMIT),
        name="mla",
    )(qm, km, vt)


def _post_kernel(x_ref, ost_ref, omt_ref, p_ref, gswa_ref, gmla_ref, wo_ref, gffn_ref,
                 wg_ref, wu_ref, wd_ref, gple_ref, wpg_ref, wpp_ref, gfin_ref, y_ref):
    def rms_t(xt, g_col):
        r = lax.rsqrt(jnp.mean(xt * xt, axis=0, keepdims=True) + EPS)
        return (xt * r) * g_col

    rows = x_ref.shape[0] // POST_SPLIT
    groups = [slice(j * rows, (j + 1) * rows) for j in range(POST_SPLIT)]

    hs = []
    for sl in groups:
        mixed_t = jnp.concatenate([rms_t(ost_ref[:, sl], gswa_ref[...]),
                                   rms_t(omt_ref[:, sl], gmla_ref[...])], axis=0)
        mix = jnp.dot(mixed_t.T.astype(BF16), wo_ref[...], preferred_element_type=F32)
        hs.append(x_ref[sl, :] + mix)

    gate_up = []
    for h in hs:
        n2 = _rms(h, gffn_ref[...]).astype(BF16)
        gate_up.append((jnp.dot(n2, wg_ref[...], preferred_element_type=F32),
                        jnp.dot(n2, wu_ref[...], preferred_element_type=F32)))

    for j, (gate, up) in enumerate(gate_up):
        act = (gate * jax.nn.sigmoid(gate) * up).astype(BF16)
        hs[j] = hs[j] + jnp.dot(act, wd_ref[...], preferred_element_type=F32)

    for sl, h in zip(groups, hs):
        n3 = _rms(h, gple_ref[...]).astype(BF16)
        pgate = jax.nn.sigmoid(jnp.dot(n3, wpg_ref[...], preferred_element_type=F32))
        emb = jnp.dot(p_ref[sl, :].astype(BF16), wpp_ref[...], preferred_element_type=F32)
        y_ref[sl, :] = _rms(h + pgate * emb, gfin_ref[...])


def _post_call(x, o_swa_t, o_mla_t, p, w):
    B, S, _ = x.shape
    tm = POST_TILE
    const = lambda shape: pl.BlockSpec(shape, lambda b, i: (0,) * len(shape),
                                       pipeline_mode=pl.Buffered(1))
    tok = lambda width: pl.BlockSpec((None, tm, width), lambda b, i: (b, i, 0))
    tok_t = lambda rows: pl.BlockSpec((None, rows, tm), lambda b, i: (b, 0, i))
    return pl.pallas_call(
        _post_kernel,
        grid=(B, S // tm),
        in_specs=[
            tok(D_MODEL), tok_t(SWA_Q_W), tok_t(MLA_OUT_W), tok(PLE_DIM),
            const((SWA_Q_W, 1)), const((MLA_OUT_W, 1)), const((SWA_Q_W + MLA_OUT_W, D_MODEL)),
            const((1, D_MODEL)), const((D_MODEL, D_FF)), const((D_MODEL, D_FF)), const((D_FF, D_MODEL)),
            const((1, D_MODEL)), const((D_MODEL, D_MODEL)), const((PLE_DIM, D_MODEL)),
            const((1, D_MODEL)),
        ],
        out_specs=tok(D_MODEL),
        out_shape=jax.ShapeDtypeStruct((B, S, D_MODEL), F32),
        compiler_params=pltpu.CompilerParams(
            dimension_semantics=("parallel", "parallel"), vmem_limit_bytes=VMEM_LIMIT),
        name="post",
    )(x, o_swa_t, o_mla_t, p, w["g_swa_out"], w["g_mla_out"], w["w_o"], w["g_ffn"],
      w["w_ffn_gate"], w["w_ffn_up"], w["w_ffn_down"], w["g_ple"], w["w_ple_gate"],
      w["w_ple_proj"], w["g_final"])


def _rope_tables(S):
    half = MLA_ROPE_DIM // 2
    inv = ROPE_THETA ** (-jnp.arange(half, dtype=F32) / half)
    ang = jnp.arange(S, dtype=F32)[:, None] * inv[None, :]
    cos, sin = jnp.cos(ang), jnp.sin(ang)
    ones = jnp.ones((S, MLA_NOPE_DIM), F32)
    zeros = jnp.zeros((S, MLA_NOPE_DIM), F32)
    z16 = jnp.zeros((S, half), F32)
    pad = jnp.zeros((S, LANES - MLA_NOPE_DIM - MLA_ROPE_DIM), F32)
    cos_t = jnp.concatenate([ones, cos, cos, pad], axis=1)
    sina_t = jnp.concatenate([zeros, -sin, z16, pad], axis=1)
    sinb_t = jnp.concatenate([zeros, z16, sin, pad], axis=1)
    return cos_t, sina_t, sinb_t


def _prep_weights(g_mix, w_in, g_cq, w_uq, g_ckv, w_ukv, sink, g_swa_out, g_mla_out, w_o, g_ffn,
                  w_ffn_gate, w_ffn_up, w_ffn_down, g_ple, w_ple_gate, w_ple_proj, g_final):
    row = lambda g: g.reshape(1, -1).astype(F32)
    w_in0 = w_in[0]
    kr_cols = w_in0[:, IN_W - MLA_ROPE_DIM:]
    w_in_p = jnp.concatenate([
        w_in0[:, :IN_W - MLA_ROPE_DIM],
        jnp.zeros((D_MODEL, MLA_NOPE_DIM), F32), kr_cols,
        jnp.zeros((D_MODEL, LANES - MLA_NOPE_DIM - MLA_ROPE_DIM), F32)], axis=1)
    qd = MLA_NOPE_DIM + MLA_ROPE_DIM
    w_uq_p = jnp.pad(w_uq[0].reshape(MLA_Q_RANK, MLA_HEADS, qd),
                     ((0, 0), (0, 0), (0, MLA_HEAD_PAD - qd))).reshape(MLA_Q_RANK, MLA_PAD_W)
    w_ukv3 = w_ukv[0].reshape(MLA_KV_RANK, MLA_HEADS, MLA_NOPE_DIM + MLA_V_DIM)
    w_uk_p = jnp.pad(w_ukv3[:, :, :MLA_NOPE_DIM],
                     ((0, 0), (0, 0), (0, MLA_HEAD_PAD - MLA_NOPE_DIM))).reshape(MLA_KV_RANK, MLA_PAD_W)
    w_uvt = w_ukv3[:, :, MLA_NOPE_DIM:].reshape(MLA_KV_RANK, MLA_OUT_W).T
    return {
        "g_mix": row(g_mix[0]), "w_in": w_in_p.astype(BF16),
        "g_cq": row(g_cq[0]), "w_uq": w_uq_p.astype(BF16),
        "g_ckv": row(g_ckv[0]), "w_uk": w_uk_p.astype(BF16), "w_uvt": w_uvt.astype(BF16),
        "sink": sink[0].astype(F32),
        "g_swa_out": g_swa_out[0].reshape(-1, 1).astype(F32),
        "g_mla_out": g_mla_out[0].reshape(-1, 1).astype(F32),
        "w_o": w_o[0].astype(BF16),
        "g_ffn": row(g_ffn[0]),
        "w_ffn_gate": w_ffn_gate[0].astype(BF16), "w_ffn_up": w_ffn_up[0].astype(BF16),
        "w_ffn_down": w_ffn_down[0].astype(BF16),
        "g_ple": row(g_ple[0]), "w_ple_gate": w_ple_gate[0].astype(BF16),
        "w_ple_proj": w_ple_proj[0].astype(BF16), "g_final": row(g_final),
    }


def _encoder(x, p, w):
    S = x.shape[1]
    cos, sina, sinb = _rope_tables(S)
    qs, ks, vst, qm, km, vt = _proj_call(x, cos, sina, sinb, w)
    o_swa_t = _swa_call(qs, ks, vst, w["sink"])
    o_mla_t = _mla_call(qm, km, vt)
    return _post_call(x, o_swa_t, o_mla_t, p[0], w)


def kernel(x_prompt, x_sample, p_prompt, p_sample, g_mix, w_in, g_cq, w_uq, g_ckv, w_ukv, sink,
           g_swa_out, g_mla_out, w_o, g_ffn, w_ffn_gate, w_ffn_up, w_ffn_down, g_ple, w_ple_gate,
           w_ple_proj, g_final):
    w = _prep_weights(g_mix, w_in, g_cq, w_uq, g_ckv, w_ukv, sink, g_swa_out, g_mla_out, w_o, g_ffn,
                      w_ffn_gate, w_ffn_up, w_ffn_down, g_ple, w_ple_gate, w_ple_proj, g_final)
    return (_encoder(x_prompt, p_prompt, w), _encoder(x_sample, p_sample, w))
```

```python
import functools
import math

import jax
import jax.numpy as jnp
import numpy as np
from jax import lax
from jax.experimental import pallas as pl
from jax.experimental.pallas import tpu as pltpu

F32 = jnp.float32
BF16 = jnp.bfloat16

D_MODEL = 1024
PLE_DIM = 256
WINDOW = 128
SWA_HEADS = 8
SWA_KV_HEADS = 2
SWA_GROUP = SWA_HEADS // SWA_KV_HEADS
SWA_HEAD_DIM = 64
MLA_HEADS = 8
MLA_NOPE_DIM = 64
MLA_ROPE_DIM = 32
MLA_V_DIM = 64
MLA_Q_RANK = 256
MLA_KV_RANK = 128
ROPE_THETA = 10000.0
D_FF = 2816
EPS = 1e-6
NEG_INF = -1e30

SWA_Q_W = SWA_HEADS * SWA_HEAD_DIM
SWA_KV_W = SWA_KV_HEADS * SWA_HEAD_DIM
MLA_OUT_W = MLA_HEADS * MLA_V_DIM
IN_W = SWA_Q_W + 2 * SWA_KV_W + MLA_Q_RANK + MLA_KV_RANK + MLA_ROPE_DIM

LANES = 128
MLA_HEAD_PAD = LANES
MLA_PAD_W = MLA_HEADS * MLA_HEAD_PAD
IN_W_PAD = 1280
OFF_CQ = SWA_Q_W + 2 * SWA_KV_W
OFF_CKV = OFF_CQ + MLA_Q_RANK
OFF_KR_BLK = OFF_CKV + MLA_KV_RANK

TOK_TILE = 512
MLA_TQ = 256
MLA_GROUP = 4
MLA_SUM_ROWS = 16
MLA_KV_COPY_PIECES = 8
MLA_STEP_UNROLL = 8
MLA_INIT_KEYS = 128
MLA_REF_SLACK = 40.0
SWA_TQ = 256
PROJ_SPLIT = 4
POST_TILE = 512
POST_SPLIT = 4
VMEM_LIMIT = 56 * 1024 * 1024

LOG2E = math.log2(math.e)
MLA_QSCALE = (MLA_NOPE_DIM + MLA_ROPE_DIM) ** -0.5 * LOG2E
SWA_QSCALE = SWA_HEAD_DIM ** -0.5 * LOG2E


def _rms(x, g):
    r = lax.rsqrt(jnp.mean(x * x, axis=-1, keepdims=True) + EPS)
    return (x * r) * g


def _proj_kernel(x_ref, gmix_ref, win_ref, gcq_ref, wuq_ref, gckv_ref, wuk_ref, wuvt_ref,
                 cos_ref, sina_ref, sinb_ref,
                 qs_ref, ks_ref, vs_ref, qm_ref, km_ref, vt_ref):
    rows = x_ref.shape[0] // PROJ_SPLIT
    groups = [slice(j * rows, (j + 1) * rows) for j in range(PROJ_SPLIT)]

    projs = []
    for sl in groups:
        n = _rms(x_ref[sl, :], gmix_ref[...]).astype(BF16)
        projs.append(jnp.dot(n, win_ref[...], preferred_element_type=F32))

    ups = []
    for sl, proj in zip(groups, projs):
        qs_ref[sl, :] = (proj[:, :SWA_Q_W] * SWA_QSCALE).astype(BF16)
        ks_ref[sl, :] = proj[:, SWA_Q_W:SWA_Q_W + SWA_KV_W].astype(BF16)
        vs_ref[:, sl] = proj[:, SWA_Q_W + SWA_KV_W:OFF_CQ].T.astype(BF16)
        cq = _rms(proj[:, OFF_CQ:OFF_CKV], gcq_ref[...]).astype(BF16)
        qm = jnp.dot(cq, wuq_ref[...], preferred_element_type=F32)
        ckv = _rms(proj[:, OFF_CKV:OFF_KR_BLK], gckv_ref[...]).astype(BF16)
        kn = jnp.dot(ckv, wuk_ref[...], preferred_element_type=F32)
        vt = lax.dot_general(wuvt_ref[...], ckv, (((1,), (1,)), ((), ())),
                             preferred_element_type=F32)
        vt_ref[:, sl] = vt.astype(BF16)
        ups.append((qm, kn))

    for sl, proj, (qm, kn) in zip(groups, projs, ups):
        cos, sina, sinb = cos_ref[sl, :], sina_ref[sl, :], sinb_ref[sl, :]

        def rope(t):
            return (t * cos + pltpu.roll(t, LANES - MLA_ROPE_DIM // 2, 1) * sina
                    + pltpu.roll(t, MLA_ROPE_DIM // 2, 1) * sinb)

        kr = rope(proj[:, OFF_KR_BLK:IN_W_PAD])
        for h in range(MLA_HEADS):
            hl = slice(h * MLA_HEAD_PAD, (h + 1) * MLA_HEAD_PAD)
            qm_ref[sl, hl] = (rope(qm[:, hl]) * MLA_QSCALE).astype(BF16)
            km_ref[h, sl, :] = (kn[:, hl] + kr).astype(BF16)


def _proj_call(x, cos, sina, sinb, w):
    B, S, _ = x.shape
    tm = TOK_TILE
    nt = S // tm
    const = lambda shape: pl.BlockSpec(shape, lambda b, i: (0,) * len(shape))
    tok = lambda width: pl.BlockSpec((None, tm, width), lambda b, i: (b, i, 0))
    tab = pl.BlockSpec((tm, LANES), lambda b, i: (i, 0))
    out_shape = (
        jax.ShapeDtypeStruct((B, S, SWA_Q_W), BF16),
        jax.ShapeDtypeStruct((B, S, SWA_KV_W), BF16),
        jax.ShapeDtypeStruct((B, SWA_KV_W, S), BF16),
        jax.ShapeDtypeStruct((B, S, MLA_PAD_W), BF16),
        jax.ShapeDtypeStruct((B, MLA_HEADS, S, MLA_HEAD_PAD), BF16),
        jax.ShapeDtypeStruct((B, nt, MLA_OUT_W, tm), BF16),
    )
    return pl.pallas_call(
        _proj_kernel,
        grid=(B, nt),
        in_specs=[
            tok(D_MODEL), const((1, D_MODEL)), const((D_MODEL, IN_W_PAD)),
            const((1, MLA_Q_RANK)), const((MLA_Q_RANK, MLA_PAD_W)),
            const((1, MLA_KV_RANK)), const((MLA_KV_RANK, MLA_PAD_W)), const((MLA_OUT_W, MLA_KV_RANK)),
            tab, tab, tab,
        ],
        out_specs=(
            tok(SWA_Q_W), tok(SWA_KV_W),
            pl.BlockSpec((None, SWA_KV_W, tm), lambda b, i: (b, 0, i)),
            tok(MLA_PAD_W),
            pl.BlockSpec((None, MLA_HEADS, tm, MLA_HEAD_PAD), lambda b, i: (b, 0, i, 0)),
            pl.BlockSpec((None, None, MLA_OUT_W, tm), lambda b, i: (b, i, 0, 0)),
        ),
        out_shape=out_shape,
        compiler_params=pltpu.CompilerParams(
            dimension_semantics=("parallel", "parallel"), vmem_limit_bytes=VMEM_LIMIT),
        name="proj",
    )(x, w["g_mix"], w["w_in"], w["g_cq"], w["w_uq"], w["g_ckv"], w["w_uk"], w["w_uvt"],
      cos, sina, sinb)


def _swa_kernel(sink_ref, q_ref, kp_ref, kc_ref, kn_ref, vp_ref, vc_ref, vn_ref, bias_ref, o_ref):
    i = pl.program_id(1)
    last = pl.num_programs(1) - 1
    kw = jnp.concatenate([kp_ref[...], kc_ref[...], kn_ref[...]], axis=0)
    vwt = jnp.concatenate([vp_ref[...], vc_ref[...], vn_ref[...]], axis=1)
    nkeys = kw.shape[0]
    row = lax.broadcasted_iota(jnp.int32, (nkeys, 1), 0)
    row_ok = ((row >= WINDOW) | (i > 0)) & ((row < nkeys - WINDOW) | (i < last))
    ones = jnp.ones((MLA_SUM_ROWS, nkeys), BF16)

    def logits(h):
        g = h // SWA_GROUP
        qh = q_ref[:, h * SWA_HEAD_DIM:(h + 1) * SWA_HEAD_DIM]
        kh = kw[:, g * SWA_HEAD_DIM:(g + 1) * SWA_HEAD_DIM]
        s = lax.dot_general(kh, qh, (((1,), (1,)), ((), ())), preferred_element_type=F32)
        return jnp.where(row_ok, s + bias_ref[h], NEG_INF)

    def attend(h, s):
        g = h // SWA_GROUP
        sink = sink_ref[h] * LOG2E
        m = jnp.maximum(jnp.max(s, axis=0, keepdims=True), sink)
        e = jnp.exp2(s - m).astype(BF16)
        vt = jnp.concatenate([vwt[g * SWA_HEAD_DIM:(g + 1) * SWA_HEAD_DIM, :], ones], axis=0)
        o = jnp.dot(vt, e, preferred_element_type=F32)
        den = o[SWA_HEAD_DIM:SWA_HEAD_DIM + 1] + jnp.exp2(sink - m)
        o_ref[h * SWA_HEAD_DIM:(h + 1) * SWA_HEAD_DIM, :] = o[:SWA_HEAD_DIM] / den

    ahead = 3
    pending = [logits(h) for h in range(ahead)]
    for h in range(SWA_HEADS):
        if h + ahead < SWA_HEADS:
            pending.append(logits(h + ahead))
        attend(h, pending.pop(0))


def _swa_bias():
    slopes = 2.0 ** (-8.0 * np.arange(1, SWA_HEADS + 1, dtype=np.float32) / SWA_HEADS)
    c = np.arange(SWA_TQ + 2 * WINDOW)[:, None]
    a = np.arange(SWA_TQ)[None, :]
    dist = np.abs(c - WINDOW - a).astype(np.float32)
    bias = np.where(dist[None] <= WINDOW, -slopes[:, None, None] * dist[None] * np.float32(LOG2E),
                    np.float32(NEG_INF))
    return jnp.asarray(bias, dtype=F32)


def _swa_call(qs, ks, vst, sink):
    B, S, _ = qs.shape
    tq = SWA_TQ
    nq = S // tq
    r = tq // WINDOW
    nhalo = S // WINDOW
    cur = lambda width: pl.BlockSpec((None, tq, width), lambda b, i: (b, i, 0))
    prev = pl.BlockSpec((None, WINDOW, SWA_KV_W), lambda b, i: (b, jnp.maximum(i * r - 1, 0), 0))
    nxt = pl.BlockSpec((None, WINDOW, SWA_KV_W),
                       lambda b, i: (b, jnp.minimum((i + 1) * r, nhalo - 1), 0))
    vprev = pl.BlockSpec((None, SWA_KV_W, WINDOW), lambda b, i: (b, 0, jnp.maximum(i * r - 1, 0)))
    vcur = pl.BlockSpec((None, SWA_KV_W, tq), lambda b, i: (b, 0, i))
    vnxt = pl.BlockSpec((None, SWA_KV_W, WINDOW),
                        lambda b, i: (b, 0, jnp.minimum((i + 1) * r, nhalo - 1)))
    bias = _swa_bias()
    return pl.pallas_call(
        _swa_kernel,
        grid=(B, nq),
        in_specs=[
            pl.BlockSpec(memory_space=pltpu.SMEM),
            cur(SWA_Q_W), prev, cur(SWA_KV_W), nxt, vprev, vcur, vnxt,
            pl.BlockSpec(bias.shape, lambda b, i: (0, 0, 0)),
        ],
        out_specs=pl.BlockSpec((None, SWA_Q_W, tq), lambda b, i: (b, 0, i)),
        out_shape=jax.ShapeDtypeStruct((B, SWA_Q_W, S), F32),
        compiler_params=pltpu.CompilerParams(
            dimension_semantics=("parallel", "parallel"), vmem_limit_bytes=VMEM_LIMIT),
        name="swa",
    )(sink, qs, ks, ks, ks, vst, vst, vst, bias)


def _mla_kv_copies(k_hbm, vt_hbm, k_ref, vt_ref, sem):
    b, h = pl.program_id(0), pl.program_id(1)
    G = MLA_GROUP
    nk, _, tk = vt_ref.shape
    npiece = MLA_KV_COPY_PIECES
    cpp = nk // npiece
    copies = []
    for j in range(npiece):
        rows = pl.ds(j * cpp * tk, cpp * tk)
        copies.append(pltpu.make_async_copy(
            k_hbm.at[b, pl.ds(h * G, G), rows, :], k_ref.at[:, rows, :], sem.at[0, j]))
        chunks = pl.ds(j * cpp, cpp)
        copies.append(pltpu.make_async_copy(
            vt_hbm.at[b, chunks, pl.ds(h * (G * MLA_V_DIM), G * MLA_V_DIM), :],
            vt_ref.at[chunks], sem.at[1, j]))
    return copies


def _mla_kernel(q_ref, k_hbm, vt_hbm, o_ref, k_ref, vt_ref, sem, s_scr, p_scr, acc_scr):
    @pl.when(pl.program_id(2) == 0)
    def _():
        copies = _mla_kv_copies(k_hbm, vt_hbm, k_ref, vt_ref, sem)
        for cp in copies:
            cp.start()
        for cp in copies:
            cp.wait()

    nk, _, tk = vt_ref.shape
    tq = q_ref.shape[0]
    G = MLA_GROUP
    qs = [q_ref[:, g * MLA_HEAD_PAD:(g + 1) * MLA_HEAD_PAD] for g in range(G)]

    def scores(c, slot):
        rows = pl.ds(pl.multiple_of(c * tk, tk), tk)
        cms = []
        for g in range(G):
            k = k_ref[g, rows, :]
            s = lax.dot_general(k, qs[g], (((1,), (1,)), ((), ())), preferred_element_type=F32)
            s_scr[slot, g] = s
            cms.append(jnp.max(s, axis=0, keepdims=True))
        return tuple(cms)

    def softmax(slot, cm, m):
        m_out, alphas = [], []
        for g in range(G):
            m_new = jnp.maximum(m[g], cm[g])
            p_scr[slot, g] = jnp.exp2(s_scr[slot, g] - m_new).astype(BF16)
            m_out.append(m_new)
            alphas.append(jnp.exp2(m[g] - m_new))
        return tuple(m_out), tuple(alphas)

    ones = jnp.ones((MLA_SUM_ROWS, tk), BF16)

    def values(c, slot, alpha):
        for g in range(G):
            vt = jnp.concatenate([vt_ref[c, g * MLA_V_DIM:(g + 1) * MLA_V_DIM, :], ones], axis=0)
            pv = jnp.dot(vt, p_scr[slot, g], preferred_element_type=F32)
            acc_scr[g] = alpha[g] * acc_scr[g] + pv

    def step(c, cur, carry, *, with_scores=True):
        cm, m, alpha_prev = carry
        nxt = 1 - cur
        cm_next = scores(c + 1, nxt) if with_scores else cm
        m, alpha = softmax(cur, cm, m)
        values(c - 1, nxt, alpha_prev)
        return cm_next, m, alpha

    def finish():
        for g in range(G):
            acc = acc_scr[g]
            o_ref[g * MLA_V_DIM:(g + 1) * MLA_V_DIM, :] = (
                acc[:MLA_V_DIM] / acc[MLA_V_DIM:MLA_V_DIM + 1])

    def robust():
        acc_scr[...] = jnp.zeros_like(acc_scr)
        cm = scores(0, 0)
        m0 = tuple(jnp.full((1, tq), NEG_INF, F32) for _ in range(G))
        cm_next = scores(1, 1)
        m, alpha = softmax(0, cm, m0)
        carry = (cm_next, m, alpha)

        def pair(i, carry):
            c = 2 * i + 1
            carry = step(c, 1, carry)
            return step(c + 1, 0, carry)

        carry = lax.fori_loop(0, (nk - 2) // 2, pair, carry)
        _, _, alpha = step(nk - 1, 1, carry, with_scores=False)
        values(nk - 1, 1, alpha)
        finish()

    def fast_step(c, cur, carry, *, with_values=True):
        ref, cm1, cm2, a_cur, over = carry
        nxt = 1 - cur
        start = (c + 1) * tk
        rows = pl.ds(start if isinstance(c, int) else pl.multiple_of(start, tk), tk)
        ref_n, cm_n, a_n, over_n = [], [], [], []
        for g in range(G):
            r = jnp.maximum(ref[g], cm2[g])
            k = k_ref[g, rows, :]
            s = lax.dot_general(k, qs[g], (((1,), (1,)), ((), ())), preferred_element_type=F32)
            p_scr[nxt, g] = jnp.exp2(s - r).astype(BF16)
            cm = jnp.max(s, axis=0, keepdims=True)
            ref_n.append(r)
            cm_n.append(cm)
            a_n.append(jnp.exp2(ref[g] - r))
            over_n.append(jnp.maximum(over[g], cm - r))
        if with_values:
            values(c, cur, a_cur)
        return tuple(ref_n), tuple(cm_n), cm1, tuple(a_n), tuple(over_n)

    def fast():
        acc_scr[...] = jnp.zeros_like(acc_scr)
        ref0 = []
        for g in range(G):
            k = k_ref[g, pl.ds(0, MLA_INIT_KEYS), :]
            s = lax.dot_general(k, qs[g], (((1,), (1,)), ((), ())), preferred_element_type=F32)
            ref0.append(jnp.max(s, axis=0, keepdims=True))
        ref0 = tuple(ref0)
        lowest = tuple(jnp.full((1, tq), NEG_INF, F32) for _ in range(G))
        one = tuple(jnp.ones((1, tq), F32) for _ in range(G))
        zero = tuple(jnp.zeros((1, tq), F32) for _ in range(G))
        carry = fast_step(-1, 1, (ref0, ref0, lowest, one, zero), with_values=False)

        unroll = MLA_STEP_UNROLL

        def body(i, carry):
            for u in range(unroll):
                carry = fast_step(unroll * i + u, u % 2, carry)
            return carry

        nbody = (nk - 1) // unroll if nk > 2 * unroll else 0
        carry = lax.fori_loop(0, nbody, body, carry)
        for c in range(nbody * unroll, nk - 1):
            carry = fast_step(c, c % 2, carry)
        _, _, _, a_last, over = carry
        values(nk - 1, (nk - 1) % 2, a_last)
        finish()
        worst = over[0]
        for g in range(1, G):
            worst = jnp.maximum(worst, over[g])
        return jnp.max(worst)

    worst = fast()

    @pl.when(worst > MLA_REF_SLACK)
    def _():
        robust()


def _mla_call(qm, km, vt):
    B, S, _ = qm.shape
    tq, G = MLA_TQ, MLA_GROUP
    nk, tk = vt.shape[1], vt.shape[3]
    assert nk % 2 == 0 and nk % MLA_KV_COPY_PIECES == 0
    return pl.pallas_call(
        _mla_kernel,
        grid=(B, MLA_HEADS // G, S // tq),
        in_specs=[
            pl.BlockSpec((None, tq, G * MLA_HEAD_PAD), lambda b, h, i: (b, i, h)),
            pl.BlockSpec(memory_space=pl.ANY),
            pl.BlockSpec(memory_space=pl.ANY),
        ],
        out_specs=pl.BlockSpec((None, G * MLA_V_DIM, tq), lambda b, h, i: (b, h, i)),
        out_shape=jax.ShapeDtypeStruct((B, MLA_OUT_W, S), F32),
        scratch_shapes=[
            pltpu.VMEM((G, S, MLA_HEAD_PAD), BF16),
            pltpu.VMEM((nk, G * MLA_V_DIM, tk), BF16),
            pltpu.SemaphoreType.DMA((2, MLA_KV_COPY_PIECES)),
            pltpu.VMEM((2, G, tk, tq), F32),
            pltpu.VMEM((2, G, tk, tq), BF16),
            pltpu.VMEM((G, MLA_V_DIM + MLA_SUM_ROWS, tq), F32),
        ],
        compiler_params=pltpu.CompilerParams(
            dimension_semantics=("arbitrary", "arbitrary", "arbitrary"),
            vmem_limit_bytes=VMEM_LIMIT),
        name="mla",
    )(qm, km, vt)


def _post_kernel(x_ref, ost_ref, omt_ref, p_ref, gswa_ref, gmla_ref, wo_ref, gffn_ref,
                 wg_ref, wu_ref, wd_ref, gple_ref, wpg_ref, wpp_ref, gfin_ref, y_ref):
    def rms_t(xt, g_col):
        r = lax.rsqrt(jnp.mean(xt * xt, axis=0, keepdims=True) + EPS)
        return (xt * r) * g_col

    rows = x_ref.shape[0] // POST_SPLIT
    groups = [slice(j * rows, (j + 1) * rows) for j in range(POST_SPLIT)]

    hs = []
    for sl in groups:
        mixed_t = jnp.concatenate([rms_t(ost_ref[:, sl], gswa_ref[...]),
                                   rms_t(omt_ref[:, sl], gmla_ref[...])], axis=0)
        mix = jnp.dot(mixed_t.T.astype(BF16), wo_ref[...], preferred_element_type=F32)
        hs.append(x_ref[sl, :] + mix)

    gate_up = []
    for h in hs:
        n2 = _rms(h, gffn_ref[...]).astype(BF16)
        gate_up.append((jnp.dot(n2, wg_ref[...], preferred_element_type=F32),
                        jnp.dot(n2, wu_ref[...], preferred_element_type=F32)))

    for j, (gate, up) in enumerate(gate_up):
        act = (gate * jax.nn.sigmoid(gate) * up).astype(BF16)
        hs[j] = hs[j] + jnp.dot(act, wd_ref[...], preferred_element_type=F32)

    for sl, h in zip(groups, hs):
        n3 = _rms(h, gple_ref[...]).astype(BF16)
        pgate = jax.nn.sigmoid(jnp.dot(n3, wpg_ref[...], preferred_element_type=F32))
        emb = jnp.dot(p_ref[sl, :].astype(BF16), wpp_ref[...], preferred_element_type=F32)
        y_ref[sl, :] = _rms(h + pgate * emb, gfin_ref[...])


def _post_call(x, o_swa_t, o_mla_t, p, w):
    B, S, _ = x.shape
    tm = POST_TILE
    const = lambda shape: pl.BlockSpec(shape, lambda b, i: (0,) * len(shape),
                                       pipeline_mode=pl.Buffered(1))
    tok = lambda width: pl.BlockSpec((None, tm, width), lambda b, i: (b, i, 0))
    tok_t = lambda rows: pl.BlockSpec((None, rows, tm), lambda b, i: (b, 0, i))
    return pl.pallas_call(
        _post_kernel,
        grid=(B, S // tm),
        in_specs=[
            tok(D_MODEL), tok_t(SWA_Q_W), tok_t(MLA_OUT_W), tok(PLE_DIM),
            const((SWA_Q_W, 1)), const((MLA_OUT_W, 1)), const((SWA_Q_W + MLA_OUT_W, D_MODEL)),
            const((1, D_MODEL)), const((D_MODEL, D_FF)), const((D_MODEL, D_FF)), const((D_FF, D_MODEL)),
            const((1, D_MODEL)), const((D_MODEL, D_MODEL)), const((PLE_DIM, D_MODEL)),
            const((1, D_MODEL)),
        ],
        out_specs=tok(D_MODEL),
        out_shape=jax.ShapeDtypeStruct((B, S, D_MODEL), F32),
        compiler_params=pltpu.CompilerParams(
            dimension_semantics=("parallel", "parallel"), vmem_limit_bytes=VMEM_LIMIT),
        name="post",
    )(x, o_swa_t, o_mla_t, p, w["g_swa_out"], w["g_mla_out"], w["w_o"], w["g_ffn"],
      w["w_ffn_gate"], w["w_ffn_up"], w["w_ffn_down"], w["g_ple"], w["w_ple_gate"],
      w["w_ple_proj"], w["g_final"])


def _rope_tables(S):
    half = MLA_ROPE_DIM // 2
    inv = ROPE_THETA ** (-jnp.arange(half, dtype=F32) / half)
    ang = jnp.arange(S, dtype=F32)[:, None] * inv[None, :]
    cos, sin = jnp.cos(ang), jnp.sin(ang)
    ones = jnp.ones((S, MLA_NOPE_DIM), F32)
    zeros = jnp.zeros((S, MLA_NOPE_DIM), F32)
    z16 = jnp.zeros((S, half), F32)
    pad = jnp.zeros((S, LANES - MLA_NOPE_DIM - MLA_ROPE_DIM), F32)
    cos_t = jnp.concatenate([ones, cos, cos, pad], axis=1)
    sina_t = jnp.concatenate([zeros, -sin, z16, pad], axis=1)
    sinb_t = jnp.concatenate([zeros, z16, sin, pad], axis=1)
    return cos_t, sina_t, sinb_t


def _prep_weights(g_mix, w_in, g_cq, w_uq, g_ckv, w_ukv, sink, g_swa_out, g_mla_out, w_o, g_ffn,
                  w_ffn_gate, w_ffn_up, w_ffn_down, g_ple, w_ple_gate, w_ple_proj, g_final):
    row = lambda g: g.reshape(1, -1).astype(F32)
    w_in0 = w_in[0]
    kr_cols = w_in0[:, IN_W - MLA_ROPE_DIM:]
    w_in_p = jnp.concatenate([
        w_in0[:, :IN_W - MLA_ROPE_DIM],
        jnp.zeros((D_MODEL, MLA_NOPE_DIM), F32), kr_cols,
        jnp.zeros((D_MODEL, LANES - MLA_NOPE_DIM - MLA_ROPE_DIM), F32)], axis=1)
    qd = MLA_NOPE_DIM + MLA_ROPE_DIM
    w_uq_p = jnp.pad(w_uq[0].reshape(MLA_Q_RANK, MLA_HEADS, qd),
                     ((0, 0), (0, 0), (0, MLA_HEAD_PAD - qd))).reshape(MLA_Q_RANK, MLA_PAD_W)
    w_ukv3 = w_ukv[0].reshape(MLA_KV_RANK, MLA_HEADS, MLA_NOPE_DIM + MLA_V_DIM)
    w_uk_p = jnp.pad(w_ukv3[:, :, :MLA_NOPE_DIM],
                     ((0, 0), (0, 0), (0, MLA_HEAD_PAD - MLA_NOPE_DIM))).reshape(MLA_KV_RANK, MLA_PAD_W)
    w_uvt = w_ukv3[:, :, MLA_NOPE_DIM:].reshape(MLA_KV_RANK, MLA_OUT_W).T
    return {
        "g_mix": row(g_mix[0]), "w_in": w_in_p.astype(BF16),
        "g_cq": row(g_cq[0]), "w_uq": w_uq_p.astype(BF16),
        "g_ckv": row(g_ckv[0]), "w_uk": w_uk_p.astype(BF16), "w_uvt": w_uvt.astype(BF16),
        "sink": sink[0].astype(F32),
        "g_swa_out": g_swa_out[0].reshape(-1, 1).astype(F32),
        "g_mla_out": g_mla_out[0].reshape(-1, 1).astype(F32),
        "w_o": w_o[0].astype(BF16),
        "g_ffn": row(g_ffn[0]),
        "w_ffn_gate": w_ffn_gate[0].astype(BF16), "w_ffn_up": w_ffn_up[0].astype(BF16),
        "w_ffn_down": w_ffn_down[0].astype(BF16),
        "g_ple": row(g_ple[0]), "w_ple_gate": w_ple_gate[0].astype(BF16),
        "w_ple_proj": w_ple_proj[0].astype(BF16), "g_final": row(g_final),
    }


def _encoder(x, p, w):
    S = x.shape[1]
    cos, sina, sinb = _rope_tables(S)
    qs, ks, vst, qm, km, vt = _proj_call(x, cos, sina, sinb, w)
    o_swa_t = _swa_call(qs, ks, vst, w["sink"])
    o_mla_t = _mla_call(qm, km, vt)
    return _post_call(x, o_swa_t, o_mla_t, p[0], w)


def kernel(x_prompt, x_sample, p_prompt, p_sample, g_mix, w_in, g_cq, w_uq, g_ckv, w_ukv, sink,
           g_swa_out, g_mla_out, w_o, g_ffn, w_ffn_gate, w_ffn_up, w_ffn_down, g_ple, w_ple_gate,
           w_ple_proj, g_final):
    w = _prep_weights(g_mix, w_in, g_cq, w_uq, g_ckv, w_ukv, sink, g_swa_out, g_mla_out, w_o, g_ffn,
                      w_ffn_gate, w_ffn_up, w_ffn_down, g_ple, w_ple_gate, w_ple_proj, g_final)
    return (_encoder(x_prompt, p_prompt, w), _encoder(x_sample, p_sample, w))
```

```python
import functools
import math

import jax
import jax.numpy as jnp
import numpy as np
from jax import lax
from jax.experimental import pallas as pl
from jax.experimental.pallas import tpu as pltpu

F32 = jnp.float32
BF16 = jnp.bfloat16

D_MODEL = 1024
PLE_DIM = 256
WINDOW = 128
SWA_HEADS = 8
SWA_KV_HEADS = 2
SWA_GROUP = SWA_HEADS // SWA_KV_HEADS
SWA_HEAD_DIM = 64
MLA_HEADS = 8
MLA_NOPE_DIM = 64
MLA_ROPE_DIM = 32
MLA_V_DIM = 64
MLA_Q_RANK = 256
MLA_KV_RANK = 128
ROPE_THETA = 10000.0
D_FF = 2816
EPS = 1e-6
NEG_INF = -1e30

SWA_Q_W = SWA_HEADS * SWA_HEAD_DIM
SWA_KV_W = SWA_KV_HEADS * SWA_HEAD_DIM
MLA_OUT_W = MLA_HEADS * MLA_V_DIM
IN_W = SWA_Q_W + 2 * SWA_KV_W + MLA_Q_RANK + MLA_KV_RANK + MLA_ROPE_DIM

LANES = 128
MLA_HEAD_PAD = LANES
MLA_PAD_W = MLA_HEADS * MLA_HEAD_PAD
IN_W_PAD = 1280
OFF_CQ = SWA_Q_W + 2 * SWA_KV_W
OFF_CKV = OFF_CQ + MLA_Q_RANK
OFF_KR_BLK = OFF_CKV + MLA_KV_RANK

TOK_TILE = 512
MLA_TQ = 256
MLA_GROUP = 4
MLA_SUM_ROWS = 16
MLA_KV_COPY_PIECES = 8
MLA_STEP_UNROLL = 8
MLA_INIT_KEYS = 128
MLA_REF_SLACK = 40.0
SWA_TQ = 256
PROJ_SPLIT = 4
POST_TILE = 512
POST_SPLIT = 4
VMEM_LIMIT = 56 * 1024 * 1024

LOG2E = math.log2(math.e)
MLA_QSCALE = (MLA_NOPE_DIM + MLA_ROPE_DIM) ** -0.5 * LOG2E
SWA_QSCALE = SWA_HEAD_DIM ** -0.5 * LOG2E


def _rms(x, g):
    r = lax.rsqrt(jnp.mean(x * x, axis=-1, keepdims=True) + EPS)
    return (x * r) * g


def _proj_kernel(x_ref, gmix_ref, win_ref, gcq_ref, wuq_ref, gckv_ref, wuk_ref, wuvt_ref,
                 cos_ref, sina_ref, sinb_ref,
                 qs_ref, ks_ref, vs_ref, qm_ref, km_ref, vt_ref):
    rows = x_ref.shape[0] // PROJ_SPLIT
    groups = [slice(j * rows, (j + 1) * rows) for j in range(PROJ_SPLIT)]

    projs = []
    for sl in groups:
        n = _rms(x_ref[sl, :], gmix_ref[...]).astype(BF16)
        projs.append(jnp.dot(n, win_ref[...], preferred_element_type=F32))

    ups = []
    for sl, proj in zip(groups, projs):
        qs_ref[sl, :] = (proj[:, :SWA_Q_W] * SWA_QSCALE).astype(BF16)
        ks_ref[sl, :] = proj[:, SWA_Q_W:SWA_Q_W + SWA_KV_W].astype(BF16)
        vs_ref[:, sl] = proj[:, SWA_Q_W + SWA_KV_W:OFF_CQ].astype(BF16).T
        cq = _rms(proj[:, OFF_CQ:OFF_CKV], gcq_ref[...]).astype(BF16)
        qm = jnp.dot(cq, wuq_ref[...], preferred_element_type=F32)
        ckv = _rms(proj[:, OFF_CKV:OFF_KR_BLK], gckv_ref[...]).astype(BF16)
        kn = jnp.dot(ckv, wuk_ref[...], preferred_element_type=F32)
        vt = lax.dot_general(wuvt_ref[...], ckv, (((1,), (1,)), ((), ())),
                             preferred_element_type=F32)
        vt_ref[:, sl] = vt.astype(BF16)
        ups.append((qm, kn))

    for sl, proj, (qm, kn) in zip(groups, projs, ups):
        cos, sina, sinb = cos_ref[sl, :], sina_ref[sl, :], sinb_ref[sl, :]

        def rope(t):
            return (t * cos + pltpu.roll(t, LANES - MLA_ROPE_DIM // 2, 1) * sina
                    + pltpu.roll(t, MLA_ROPE_DIM // 2, 1) * sinb)

        kr = rope(proj[:, OFF_KR_BLK:IN_W_PAD])
        for h in range(MLA_HEADS):
            hl = slice(h * MLA_HEAD_PAD, (h + 1) * MLA_HEAD_PAD)
            qm_ref[hl, sl] = (rope(qm[:, hl]) * MLA_QSCALE).astype(BF16).T
            km_ref[h, sl, :] = (kn[:, hl] + kr).astype(BF16)


def _proj_call(x, cos, sina, sinb, w):
    B, S, _ = x.shape
    tm = TOK_TILE
    nt = S // tm
    const = lambda shape: pl.BlockSpec(shape, lambda b, i: (0,) * len(shape))
    tok = lambda width: pl.BlockSpec((None, tm, width), lambda b, i: (b, i, 0))
    tab = pl.BlockSpec((tm, LANES), lambda b, i: (i, 0))
    out_shape = (
        jax.ShapeDtypeStruct((B, S, SWA_Q_W), BF16),
        jax.ShapeDtypeStruct((B, S, SWA_KV_W), BF16),
        jax.ShapeDtypeStruct((B, SWA_KV_W, S), BF16),
        jax.ShapeDtypeStruct((B, MLA_PAD_W, S), BF16),
        jax.ShapeDtypeStruct((B, MLA_HEADS, S, MLA_HEAD_PAD), BF16),
        jax.ShapeDtypeStruct((B, nt, MLA_OUT_W, tm), BF16),
    )
    return pl.pallas_call(
        _proj_kernel,
        grid=(B, nt),
        in_specs=[
            tok(D_MODEL), const((1, D_MODEL)), const((D_MODEL, IN_W_PAD)),
            const((1, MLA_Q_RANK)), const((MLA_Q_RANK, MLA_PAD_W)),
            const((1, MLA_KV_RANK)), const((MLA_KV_RANK, MLA_PAD_W)), const((MLA_OUT_W, MLA_KV_RANK)),
            tab, tab, tab,
        ],
        out_specs=(
            tok(SWA_Q_W), tok(SWA_KV_W),
            pl.BlockSpec((None, SWA_KV_W, tm), lambda b, i: (b, 0, i)),
            pl.BlockSpec((None, MLA_PAD_W, tm), lambda b, i: (b, 0, i)),
            pl.BlockSpec((None, MLA_HEADS, tm, MLA_HEAD_PAD), lambda b, i: (b, 0, i, 0)),
            pl.BlockSpec((None, None, MLA_OUT_W, tm), lambda b, i: (b, i, 0, 0)),
        ),
        out_shape=out_shape,
        compiler_params=pltpu.CompilerParams(
            dimension_semantics=("parallel", "parallel"), vmem_limit_bytes=VMEM_LIMIT),
        name="proj",
    )(x, w["g_mix"], w["w_in"], w["g_cq"], w["w_uq"], w["g_ckv"], w["w_uk"], w["w_uvt"],
      cos, sina, sinb)


def _swa_kernel(sink_ref, q_ref, kp_ref, kc_ref, kn_ref, vp_ref, vc_ref, vn_ref, bias_ref, o_ref):
    i = pl.program_id(1)
    last = pl.num_programs(1) - 1
    kw = jnp.concatenate([kp_ref[...], kc_ref[...], kn_ref[...]], axis=0)
    vwt = jnp.concatenate([vp_ref[...], vc_ref[...], vn_ref[...]], axis=1)
    nkeys = kw.shape[0]
    row = lax.broadcasted_iota(jnp.int32, (nkeys, 1), 0)
    row_ok = ((row >= WINDOW) | (i > 0)) & ((row < nkeys - WINDOW) | (i < last))
    ones = jnp.ones((MLA_SUM_ROWS, nkeys), BF16)

    def logits(h):
        g = h // SWA_GROUP
        qh = q_ref[:, h * SWA_HEAD_DIM:(h + 1) * SWA_HEAD_DIM]
        kh = kw[:, g * SWA_HEAD_DIM:(g + 1) * SWA_HEAD_DIM]
        s = lax.dot_general(kh, qh, (((1,), (1,)), ((), ())), preferred_element_type=F32)
        return jnp.where(row_ok, s + bias_ref[h], NEG_INF)

    def attend(h, s):
        g = h // SWA_GROUP
        sink = sink_ref[h] * LOG2E
        m = jnp.maximum(jnp.max(s, axis=0, keepdims=True), sink)
        e = jnp.exp2(s - m).astype(BF16)
        vt = jnp.concatenate([vwt[g * SWA_HEAD_DIM:(g + 1) * SWA_HEAD_DIM, :], ones], axis=0)
        o = jnp.dot(vt, e, preferred_element_type=F32)
        den = o[SWA_HEAD_DIM:SWA_HEAD_DIM + 1] + jnp.exp2(sink - m)
        o_ref[h * SWA_HEAD_DIM:(h + 1) * SWA_HEAD_DIM, :] = o[:SWA_HEAD_DIM] / den

    ahead = 3
    pending = [logits(h) for h in range(ahead)]
    for h in range(SWA_HEADS):
        if h + ahead < SWA_HEADS:
            pending.append(logits(h + ahead))
        attend(h, pending.pop(0))


def _swa_bias():
    slopes = 2.0 ** (-8.0 * np.arange(1, SWA_HEADS + 1, dtype=np.float32) / SWA_HEADS)
    c = np.arange(SWA_TQ + 2 * WINDOW)[:, None]
    a = np.arange(SWA_TQ)[None, :]
    dist = np.abs(c - WINDOW - a).astype(np.float32)
    bias = np.where(dist[None] <= WINDOW, -slopes[:, None, None] * dist[None] * np.float32(LOG2E),
                    np.float32(NEG_INF))
    return jnp.asarray(bias, dtype=F32)


def _swa_call(qs, ks, vst, sink):
    B, S, _ = qs.shape
    tq = SWA_TQ
    nq = S // tq
    r = tq // WINDOW
    nhalo = S // WINDOW
    cur = lambda width: pl.BlockSpec((None, tq, width), lambda b, i: (b, i, 0))
    prev = pl.BlockSpec((None, WINDOW, SWA_KV_W), lambda b, i: (b, jnp.maximum(i * r - 1, 0), 0))
    nxt = pl.BlockSpec((None, WINDOW, SWA_KV_W),
                       lambda b, i: (b, jnp.minimum((i + 1) * r, nhalo - 1), 0))
    vprev = pl.BlockSpec((None, SWA_KV_W, WINDOW), lambda b, i: (b, 0, jnp.maximum(i * r - 1, 0)))
    vcur = pl.BlockSpec((None, SWA_KV_W, tq), lambda b, i: (b, 0, i))
    vnxt = pl.BlockSpec((None, SWA_KV_W, WINDOW),
                        lambda b, i: (b, 0, jnp.minimum((i + 1) * r, nhalo - 1)))
    bias = _swa_bias()
    return pl.pallas_call(
        _swa_kernel,
        grid=(B, nq),
        in_specs=[
            pl.BlockSpec(memory_space=pltpu.SMEM),
            cur(SWA_Q_W), prev, cur(SWA_KV_W), nxt, vprev, vcur, vnxt,
            pl.BlockSpec(bias.shape, lambda b, i: (0, 0, 0)),
        ],
        out_specs=pl.BlockSpec((None, SWA_Q_W, tq), lambda b, i: (b, 0, i)),
        out_shape=jax.ShapeDtypeStruct((B, SWA_Q_W, S), F32),
        compiler_params=pltpu.CompilerParams(
            dimension_semantics=("parallel", "parallel"), vmem_limit_bytes=VMEM_LIMIT),
        name="swa",
    )(sink, qs, ks, ks, ks, vst, vst, vst, bias)


def _mla_kv_copies(k_hbm, vt_hbm, k_ref, vt_ref, sem):
    b, h = pl.program_id(0), pl.program_id(1)
    G = MLA_GROUP
    nk, _, tk = vt_ref.shape
    npiece = MLA_KV_COPY_PIECES
    cpp = nk // npiece
    copies = []
    for j in range(npiece):
        rows = pl.ds(j * cpp * tk, cpp * tk)
        copies.append(pltpu.make_async_copy(
            k_hbm.at[b, pl.ds(h * G, G), rows, :], k_ref.at[:, rows, :], sem.at[0, j]))
        chunks = pl.ds(j * cpp, cpp)
        copies.append(pltpu.make_async_copy(
            vt_hbm.at[b, chunks, pl.ds(h * (G * MLA_V_DIM), G * MLA_V_DIM), :],
            vt_ref.at[chunks], sem.at[1, j]))
    return copies


def _mla_kernel(q_ref, k_hbm, vt_hbm, o_ref, k_ref, vt_ref, sem, s_scr, p_scr, acc_scr):
    @pl.when(pl.program_id(2) == 0)
    def _():
        copies = _mla_kv_copies(k_hbm, vt_hbm, k_ref, vt_ref, sem)
        for cp in copies:
            cp.start()
        for cp in copies:
            cp.wait()

    nk, _, tk = vt_ref.shape
    tq = q_ref.shape[1]
    G = MLA_GROUP
    qs = [q_ref[g * MLA_HEAD_PAD:(g + 1) * MLA_HEAD_PAD, :] for g in range(G)]

    def scores(c, slot):
        rows = pl.ds(pl.multiple_of(c * tk, tk), tk)
        cms = []
        for g in range(G):
            k = k_ref[g, rows, :]
            s = jnp.dot(k, qs[g], preferred_element_type=F32)
            s_scr[slot, g] = s
            cms.append(jnp.max(s, axis=0, keepdims=True))
        return tuple(cms)

    def softmax(slot, cm, m):
        m_out, alphas = [], []
        for g in range(G):
            m_new = jnp.maximum(m[g], cm[g])
            p_scr[slot, g] = jnp.exp2(s_scr[slot, g] - m_new).astype(BF16)
            m_out.append(m_new)
            alphas.append(jnp.exp2(m[g] - m_new))
        return tuple(m_out), tuple(alphas)

    ones = jnp.ones((MLA_SUM_ROWS, tk), BF16)

    def values(c, slot, alpha, heads=range(G)):
        for g in heads:
            vt = jnp.concatenate([vt_ref[c, g * MLA_V_DIM:(g + 1) * MLA_V_DIM, :], ones], axis=0)
            pv = jnp.dot(vt, p_scr[slot, g], preferred_element_type=F32)
            acc_scr[g] = alpha[g] * acc_scr[g] + pv

    def step(c, cur, carry, *, with_scores=True):
        cm, m, alpha_prev = carry
        nxt = 1 - cur
        cm_next = scores(c + 1, nxt) if with_scores else cm
        m, alpha = softmax(cur, cm, m)
        values(c - 1, nxt, alpha_prev)
        return cm_next, m, alpha

    def finish():
        for g in range(G):
            acc = acc_scr[g]
            o_ref[g * MLA_V_DIM:(g + 1) * MLA_V_DIM, :] = (
                acc[:MLA_V_DIM] / acc[MLA_V_DIM:MLA_V_DIM + 1])

    def robust():
        acc_scr[...] = jnp.zeros_like(acc_scr)
        cm = scores(0, 0)
        m0 = tuple(jnp.full((1, tq), NEG_INF, F32) for _ in range(G))
        cm_next = scores(1, 1)
        m, alpha = softmax(0, cm, m0)
        carry = (cm_next, m, alpha)

        def pair(i, carry):
            c = 2 * i + 1
            carry = step(c, 1, carry)
            return step(c + 1, 0, carry)

        carry = lax.fori_loop(0, (nk - 2) // 2, pair, carry)
        _, _, alpha = step(nk - 1, 1, carry, with_scores=False)
        values(nk - 1, 1, alpha)
        finish()

    def fast_step(c, cur, carry, *, with_values=True):
        ref, cm1, cm2, a_cur, over = carry
        nxt = 1 - cur
        start = (c + 1) * tk
        rows = pl.ds(start if isinstance(c, int) else pl.multiple_of(start, tk), tk)
        ref_n, cm_n, a_n, over_n = [], [], [], []
        for g in range(G):
            r = jnp.maximum(ref[g], cm2[g])
            k = k_ref[g, rows, :]
            s = jnp.dot(k, qs[g], preferred_element_type=F32)
            p_scr[nxt, g] = jnp.exp2(s - r).astype(BF16)
            cm = jnp.max(s, axis=0, keepdims=True)
            ref_n.append(r)
            cm_n.append(cm)
            a_n.append(jnp.exp2(ref[g] - r))
            over_n.append(jnp.maximum(over[g], cm - r))
            if with_values:
                values(c, cur, a_cur, heads=(g,))
        return tuple(ref_n), tuple(cm_n), cm1, tuple(a_n), tuple(over_n)

    def fast():
        acc_scr[...] = jnp.zeros_like(acc_scr)
        ref0 = []
        for g in range(G):
            k = k_ref[g, pl.ds(0, MLA_INIT_KEYS), :]
            s = jnp.dot(k, qs[g], preferred_element_type=F32)
            ref0.append(jnp.max(s, axis=0, keepdims=True))
        ref0 = tuple(ref0)
        lowest = tuple(jnp.full((1, tq), NEG_INF, F32) for _ in range(G))
        one = tuple(jnp.ones((1, tq), F32) for _ in range(G))
        zero = tuple(jnp.zeros((1, tq), F32) for _ in range(G))
        carry = fast_step(-1, 1, (ref0, ref0, lowest, one, zero), with_values=False)

        unroll = MLA_STEP_UNROLL

        def body(i, carry):
            for u in range(unroll):
                carry = fast_step(unroll * i + u, u % 2, carry)
            return carry

        nbody = (nk - 1) // unroll if nk > 2 * unroll else 0
        carry = lax.fori_loop(0, nbody, body, carry)
        for c in range(nbody * unroll, nk - 1):
            carry = fast_step(c, c % 2, carry)
        _, _, _, a_last, over = carry
        values(nk - 1, (nk - 1) % 2, a_last)
        finish()
        worst = over[0]
        for g in range(1, G):
            worst = jnp.maximum(worst, over[g])
        return jnp.max(worst)

    worst = fast()

    @pl.when(worst > MLA_REF_SLACK)
    def _():
        robust()


def _mla_call(qmt, km, vt):
    B, _, S = qmt.shape
    tq, G = MLA_TQ, MLA_GROUP
    nk, tk = vt.shape[1], vt.shape[3]
    assert nk % 2 == 0 and nk % MLA_KV_COPY_PIECES == 0
    return pl.pallas_call(
        _mla_kernel,
        grid=(B, MLA_HEADS // G, S // tq),
        in_specs=[
            pl.BlockSpec((None, G * MLA_HEAD_PAD, tq), lambda b, h, i: (b, h, i)),
            pl.BlockSpec(memory_space=pl.ANY),
            pl.BlockSpec(memory_space=pl.ANY),
        ],
        out_specs=pl.BlockSpec((None, G * MLA_V_DIM, tq), lambda b, h, i: (b, h, i)),
        out_shape=jax.ShapeDtypeStruct((B, MLA_OUT_W, S), F32),
        scratch_shapes=[
            pltpu.VMEM((G, S, MLA_HEAD_PAD), BF16),
            pltpu.VMEM((nk, G * MLA_V_DIM, tk), BF16),
            pltpu.SemaphoreType.DMA((2, MLA_KV_COPY_PIECES)),
            pltpu.VMEM((2, G, tk, tq), F32),
            pltpu.VMEM((2, G, tk, tq), BF16),
            pltpu.VMEM((G, MLA_V_DIM + MLA_SUM_ROWS, tq), F32),
        ],
        compiler_params=pltpu.CompilerParams(
            dimension_semantics=("arbitrary", "arbitrary", "arbitrary"),
            vmem_limit_bytes=VMEM_LIMIT),
        name="mla",
    )(qmt, km, vt)


def _post_kernel(x_ref, ost_ref, omt_ref, p_ref, gswa_ref, gmla_ref, wo_ref, gffn_ref,
                 wg_ref, wu_ref, wd_ref, gple_ref, wpg_ref, wpp_ref, gfin_ref, y_ref):
    def rms_t(xt, g_col):
        r = lax.rsqrt(jnp.mean(xt * xt, axis=0, keepdims=True) + EPS)
        return (xt * r) * g_col

    rows = x_ref.shape[0] // POST_SPLIT
    groups = [slice(j * rows, (j + 1) * rows) for j in range(POST_SPLIT)]

    hs = []
    for sl in groups:
        mixed_t = jnp.concatenate([rms_t(ost_ref[:, sl], gswa_ref[...]),
                                   rms_t(omt_ref[:, sl], gmla_ref[...])], axis=0)
        mix = jnp.dot(mixed_t.T.astype(BF16), wo_ref[...], preferred_element_type=F32)
        hs.append(x_ref[sl, :] + mix)

    gate_up = []
    for h in hs:
        n2 = _rms(h, gffn_ref[...]).astype(BF16)
        gate_up.append((jnp.dot(n2, wg_ref[...], preferred_element_type=F32),
                        jnp.dot(n2, wu_ref[...], preferred_element_type=F32)))

    for j, (gate, up) in enumerate(gate_up):
        act = (gate * jax.nn.sigmoid(gate) * up).astype(BF16)
        hs[j] = hs[j] + jnp.dot(act, wd_ref[...], preferred_element_type=F32)

    for sl, h in zip(groups, hs):
        n3 = _rms(h, gple_ref[...]).astype(BF16)
        pgate = jax.nn.sigmoid(jnp.dot(n3, wpg_ref[...], preferred_element_type=F32))
        emb = jnp.dot(p_ref[sl, :].astype(BF16), wpp_ref[...], preferred_element_type=F32)
        y_ref[sl, :] = _rms(h + pgate * emb, gfin_ref[...])


def _post_call(x, o_swa_t, o_mla_t, p, w):
    B, S, _ = x.shape
    tm = POST_TILE
    const = lambda shape: pl.BlockSpec(shape, lambda b, i: (0,) * len(shape),
                                       pipeline_mode=pl.Buffered(1))
    tok = lambda width: pl.BlockSpec((None, tm, width), lambda b, i: (b, i, 0))
    tok_t = lambda rows: pl.BlockSpec((None, rows, tm), lambda b, i: (b, 0, i))
    return pl.pallas_call(
        _post_kernel,
        grid=(B, S // tm),
        in_specs=[
            tok(D_MODEL), tok_t(SWA_Q_W), tok_t(MLA_OUT_W), tok(PLE_DIM),
            const((SWA_Q_W, 1)), const((MLA_OUT_W, 1)), const((SWA_Q_W + MLA_OUT_W, D_MODEL)),
            const((1, D_MODEL)), const((D_MODEL, D_FF)), const((D_MODEL, D_FF)), const((D_FF, D_MODEL)),
            const((1, D_MODEL)), const((D_MODEL, D_MODEL)), const((PLE_DIM, D_MODEL)),
            const((1, D_MODEL)),
        ],
        out_specs=tok(D_MODEL),
        out_shape=jax.ShapeDtypeStruct((B, S, D_MODEL), F32),
        compiler_params=pltpu.CompilerParams(
            dimension_semantics=("parallel", "parallel"), vmem_limit_bytes=VMEM_LIMIT),
        name="post",
    )(x, o_swa_t, o_mla_t, p, w["g_swa_out"], w["g_mla_out"], w["w_o"], w["g_ffn"],
      w["w_ffn_gate"], w["w_ffn_up"], w["w_ffn_down"], w["g_ple"], w["w_ple_gate"],
      w["w_ple_proj"], w["g_final"])


def _rope_tables(S):
    half = MLA_ROPE_DIM // 2
    inv = ROPE_THETA ** (-jnp.arange(half, dtype=F32) / half)
    ang = jnp.arange(S, dtype=F32)[:, None] * inv[None, :]
    cos, sin = jnp.cos(ang), jnp.sin(ang)
    ones = jnp.ones((S, MLA_NOPE_DIM), F32)
    zeros = jnp.zeros((S, MLA_NOPE_DIM), F32)
    z16 = jnp.zeros((S, half), F32)
    pad = jnp.zeros((S, LANES - MLA_NOPE_DIM - MLA_ROPE_DIM), F32)
    cos_t = jnp.concatenate([ones, cos, cos, pad], axis=1)
    sina_t = jnp.concatenate([zeros, -sin, z16, pad], axis=1)
    sinb_t = jnp.concatenate([zeros, z16, sin, pad], axis=1)
    return cos_t, sina_t, sinb_t


def _prep_weights(g_mix, w_in, g_cq, w_uq, g_ckv, w_ukv, sink, g_swa_out, g_mla_out, w_o, g_ffn,
                  w_ffn_gate, w_ffn_up, w_ffn_down, g_ple, w_ple_gate, w_ple_proj, g_final):
    row = lambda g: g.reshape(1, -1).astype(F32)
    w_in0 = w_in[0]
    kr_cols = w_in0[:, IN_W - MLA_ROPE_DIM:]
    w_in_p = jnp.concatenate([
        w_in0[:, :IN_W - MLA_ROPE_DIM],
        jnp.zeros((D_MODEL, MLA_NOPE_DIM), F32), kr_cols,
        jnp.zeros((D_MODEL, LANES - MLA_NOPE_DIM - MLA_ROPE_DIM), F32)], axis=1)
    qd = MLA_NOPE_DIM + MLA_ROPE_DIM
    w_uq_p = jnp.pad(w_uq[0].reshape(MLA_Q_RANK, MLA_HEADS, qd),
                     ((0, 0), (0, 0), (0, MLA_HEAD_PAD - qd))).reshape(MLA_Q_RANK, MLA_PAD_W)
    w_ukv3 = w_ukv[0].reshape(MLA_KV_RANK, MLA_HEADS, MLA_NOPE_DIM + MLA_V_DIM)
    w_uk_p = jnp.pad(w_ukv3[:, :, :MLA_NOPE_DIM],
                     ((0, 0), (0, 0), (0, MLA_HEAD_PAD - MLA_NOPE_DIM))).reshape(MLA_KV_RANK, MLA_PAD_W)
    w_uvt = w_ukv3[:, :, MLA_NOPE_DIM:].reshape(MLA_KV_RANK, MLA_OUT_W).T
    return {
        "g_mix": row(g_mix[0]), "w_in": w_in_p.astype(BF16),
        "g_cq": row(g_cq[0]), "w_uq": w_uq_p.astype(BF16),
        "g_ckv": row(g_ckv[0]), "w_uk": w_uk_p.astype(BF16), "w_uvt": w_uvt.astype(BF16),
        "sink": sink[0].astype(F32),
        "g_swa_out": g_swa_out[0].reshape(-1, 1).astype(F32),
        "g_mla_out": g_mla_out[0].reshape(-1, 1).astype(F32),
        "w_o": w_o[0].astype(BF16),
        "g_ffn": row(g_ffn[0]),
        "w_ffn_gate": w_ffn_gate[0].astype(BF16), "w_ffn_up": w_ffn_up[0].astype(BF16),
        "w_ffn_down": w_ffn_down[0].astype(BF16),
        "g_ple": row(g_ple[0]), "w_ple_gate": w_ple_gate[0].astype(BF16),
        "w_ple_proj": w_ple_proj[0].astype(BF16), "g_final": row(g_final),
    }


def _encoder(x, p, w):
    S = x.shape[1]
    cos, sina, sinb = _rope_tables(S)
    qs, ks, vst, qmt, km, vt = _proj_call(x, cos, sina, sinb, w)
    o_swa_t = _swa_call(qs, ks, vst, w["sink"])
    o_mla_t = _mla_call(qmt, km, vt)
    return _post_call(x, o_swa_t, o_mla_t, p[0], w)


def kernel(x_prompt, x_sample, p_prompt, p_sample, g_mix, w_in, g_cq, w_uq, g_ckv, w_ukv, sink,
           g_swa_out, g_mla_out, w_o, g_ffn, w_ffn_gate, w_ffn_up, w_ffn_down, g_ple, w_ple_gate,
           w_ple_proj, g_final):
    w = _prep_weights(g_mix, w_in, g_cq, w_uq, g_ckv, w_ukv, sink, g_swa_out, g_mla_out, w_o, g_ffn,
                      w_ffn_gate, w_ffn_up, w_ffn_down, g_ple, w_ple_gate, w_ple_proj, g_final)
    return (_encoder(x_prompt, p_prompt, w), _encoder(x_sample, p_sample, w))
```
